```python
import math
import jax, jax.numpy as jnp
from jax import lax
import numpy as np

D_MODEL = 1024
BATCH = 16
SEQ = 2048
DEPTH = 2

DA_HEADS = 4
DA_HEAD_DIM = 64
DA_QK_WIDTH = DA_HEADS * 2 * DA_HEAD_DIM
DA_V_WIDTH = DA_HEADS * 2 * DA_HEAD_DIM
MLA_HEADS = 8
MLA_Q_LORA = 384
MLA_KV_LORA = 256
MLA_NOPE = 64
MLA_ROPE = 32
MLA_V = 64
ROPE_THETA = 10000.0
SC_WIDTH = 512
SC_KERNEL = 3
CF_WIDTH = 512
CF_KERNEL = 31
N_BRANCHES = 4
FFN_DIM = 2816
FFN_KERNEL = 3
N_MOD = 6
Q_BLOCK = 128
NORM_EPS = 1e-6
LN_EPS = 1e-5
IN_SIZES = (DA_QK_WIDTH, DA_QK_WIDTH, DA_V_WIDTH,
            MLA_Q_LORA, MLA_KV_LORA, MLA_ROPE,
            SC_WIDTH, SC_WIDTH, SC_WIDTH,
            2 * CF_WIDTH,
            N_BRANCHES * D_MODEL)
IN_WIDTH = (3 * DA_QK_WIDTH + MLA_Q_LORA + MLA_KV_LORA + MLA_ROPE
            + 3 * SC_WIDTH + 2 * CF_WIDTH + N_BRANCHES * D_MODEL)

kernel_name = "hybrid_gated_parallel_mixers"


def rmsnorm(x, g):
    xf = x.astype(jnp.float32)
    y = xf * lax.rsqrt(jnp.mean(xf * xf, axis=-1, keepdims=True) + NORM_EPS)
    return (y * g.astype(jnp.float32)).astype(x.dtype)


def layernorm(x, g, b):
    xf = x.astype(jnp.float32)
    mu = jnp.mean(xf, axis=-1, keepdims=True)
    var = jnp.mean(jnp.square(xf - mu), axis=-1, keepdims=True)
    y = (xf - mu) * lax.rsqrt(var + LN_EPS)
    return (y * g.astype(jnp.float32) + b.astype(jnp.float32)).astype(x.dtype)


def causal_dwconv(x, w):
    k, ch = w.shape
    return lax.conv_general_dilated(
        x, w[:, None, :].astype(x.dtype), window_strides=(1,), padding=[(k - 1, 0)],
        dimension_numbers=('NWC', 'WIO', 'NWC'), feature_group_count=ch)


def rope_tables(positions):
    half = MLA_ROPE // 2
    inv_freq = ROPE_THETA ** (-jnp.arange(half, dtype=jnp.float32) / half)
    ang = positions.astype(jnp.float32)[:, :, None, None] * inv_freq
    return jnp.cos(ang), jnp.sin(ang)


def apply_rope(t, cos, sin):
    t1, t2 = jnp.split(t.astype(jnp.float32), 2, axis=-1)
    return jnp.concatenate([t1 * cos - t2 * sin, t2 * cos + t1 * sin], axis=-1).astype(t.dtype)


def to_blocks(t):
    b, s = t.shape[:2]
    return jnp.moveaxis(t.reshape((b, s // Q_BLOCK, Q_BLOCK) + t.shape[2:]), 1, 0)


def from_blocks(t):
    t = jnp.moveaxis(t, 0, 1)
    return t.reshape((t.shape[0], -1) + t.shape[3:])


def block_starts(s):
    return jnp.arange(s // Q_BLOCK, dtype=jnp.int32) * Q_BLOCK


def causal_probs(q_blk, k, q_start, scale):
    s = jnp.einsum('bqhd,bkhd->bhqk', q_blk, k).astype(jnp.float32) * scale
    q_pos = q_start + jnp.arange(q_blk.shape[1], dtype=jnp.int32)
    k_pos = jnp.arange(k.shape[1], dtype=jnp.int32)
    mask = k_pos[None, :] <= q_pos[:, None]
    return jax.nn.softmax(jnp.where(mask, s, -jnp.inf), axis=-1)


def differential_attention(q1, q2, k1, k2, v, lam):
    scale = DA_HEAD_DIM ** -0.5

    def one_block(args):
        q1b, q2b, st = args
        p = causal_probs(q1b, k1, st, scale) - lam * causal_probs(q2b, k2, st, scale)
        return jnp.einsum('bhqk,bkhd->bqhd', p.astype(v.dtype), v)

    out = lax.map(one_block, (to_blocks(q1), to_blocks(q2), block_starts(q1.shape[1])))
    return from_blocks(out)


def softmax_attention(q, k, v):
    scale = (MLA_NOPE + MLA_ROPE) ** -0.5

    def one_block(args):
        qb, st = args
        p = causal_probs(qb, k, st, scale)
        return jnp.einsum('bhqk,bkhd->bqhd', p.astype(v.dtype), v)

    out = lax.map(one_block, (to_blocks(q), block_starts(q.shape[1])))
    return from_blocks(out)


def token_mixing(x, shift, scale, cos, sin, layer_idx, g_pre, w_in,
                 lq1, lk1, lq2, lk2, g_subln, da_w_o,
                 g_cq, w_uq, g_ckv, w_ukv, mla_w_o,
                 sc_conv, sc_w_o,
                 cf_conv, cf_conv_b, cf_ln_g, cf_ln_b, cf_w_o,
                 w_mix_out, g_post):
    bsz, s, _ = x.shape
    h = rmsnorm(x, g_pre) * (1.0 + scale) + shift
    proj = h @ w_in
    split_idx = []
    acc = 0
    for size in IN_SIZES[:-1]:
        acc += size
        split_idx.append(acc)
    (da_q, da_k, da_v, mla_cq, mla_ckv, mla_kr,
     sc_u, sc_b, sc_c, cf_in, gates) = jnp.split(proj, split_idx, axis=-1)

    da_q = da_q.reshape(bsz, s, DA_HEADS, 2, DA_HEAD_DIM)
    da_k = da_k.reshape(bsz, s, DA_HEADS, 2, DA_HEAD_DIM)
    da_v = da_v.reshape(bsz, s, DA_HEADS, 2 * DA_HEAD_DIM)
    lam_init = 0.8 - 0.6 * math.exp(-0.3 * layer_idx)
    lam = (jnp.exp(jnp.sum(lq1.astype(jnp.float32) * lk1.astype(jnp.float32)))
           - jnp.exp(jnp.sum(lq2.astype(jnp.float32) * lk2.astype(jnp.float32))) + lam_init)
    ya = differential_attention(da_q[:, :, :, 0], da_q[:, :, :, 1],
                                da_k[:, :, :, 0], da_k[:, :, :, 1], da_v, lam)
    ya = rmsnorm(ya, g_subln) * (1.0 - lam_init)
    ya = ya.reshape(bsz, s, DA_V_WIDTH) @ da_w_o

    q = (rmsnorm(mla_cq, g_cq) @ w_uq).reshape(bsz, s, MLA_HEADS, MLA_NOPE + MLA_ROPE)
    q_nope, q_rope = jnp.split(q, [MLA_NOPE], axis=-1)
    q = jnp.concatenate([q_nope, apply_rope(q_rope, cos, sin)], axis=-1)
    kv = (rmsnorm(mla_ckv, g_ckv) @ w_ukv).reshape(bsz, s, MLA_HEADS, MLA_NOPE + MLA_V)
    k_nope, v_b = jnp.split(kv, [MLA_NOPE], axis=-1)
    k_rope = apply_rope(mla_kr[:, :, None, :], cos, sin)
    k = jnp.concatenate([k_nope, jnp.broadcast_to(k_rope, (bsz, s, MLA_HEADS, MLA_ROPE))], axis=-1)
    yb = softmax_attention(q, k, v_b).reshape(bsz, s, MLA_HEADS * MLA_V) @ mla_w_o

    yc = (sc_b * causal_dwconv(sc_c * sc_u, sc_conv)) @ sc_w_o

    cf_a, cf_g = jnp.split(cf_in, 2, axis=-1)
    u = cf_a * jax.nn.sigmoid(cf_g)
    u = causal_dwconv(u, cf_conv) + cf_conv_b
    u = jax.nn.silu(layernorm(u, cf_ln_g, cf_ln_b))
    yd = u @ cf_w_o

    g = jax.nn.sigmoid(gates).reshape(bsz, s, N_BRANCHES, D_MODEL)
    merged = g[:, :, 0] * ya + g[:, :, 1] * yb + g[:, :, 2] * yc + g[:, :, 3] * yd
    return rmsnorm(merged @ w_mix_out, g_post)


def channel_mixing(x, shift, scale, g_pre, w_up, ffn_conv, w_down, g_post):
    h = rmsnorm(x, g_pre) * (1.0 + scale) + shift
    a, b = jnp.split(h @ w_up, 2, axis=-1)
    y = (jax.nn.silu(causal_dwconv(a, ffn_conv)) * b) @ w_down
    return rmsnorm(y, g_post)


def setup_inputs(seed: int = 0) -> dict:
    key = jax.random.key(seed)
    ks = iter(jax.random.split(key, 40))

    def nrm(shape, scale):
        return jax.random.normal(next(ks), shape, jnp.float32) * scale

    def gain(shape):
        return 1.0 + nrm(shape, 0.02)

    L = DEPTH
    return {
        "x": nrm((BATCH, SEQ, D_MODEL), 1.0),
        "c": nrm((BATCH, D_MODEL), 1.0),
        "positions": jnp.broadcast_to(jnp.arange(SEQ, dtype=jnp.int32), (BATCH, SEQ)),
        "w_ada": nrm((L, D_MODEL, N_MOD * D_MODEL), 0.5 * D_MODEL ** -0.5),
        "b_ada": nrm((L, N_MOD * D_MODEL), 0.01),
        "g_pre_mix": gain((L, D_MODEL)),
        "w_in": nrm((L, D_MODEL, IN_WIDTH), D_MODEL ** -0.5),
        "da_lam_q1": nrm((L, DA_HEAD_DIM), 0.1),
        "da_lam_k1": nrm((L, DA_HEAD_DIM), 0.1),
        "da_lam_q2": nrm((L, DA_HEAD_DIM), 0.1),
        "da_lam_k2": nrm((L, DA_HEAD_DIM), 0.1),
        "da_g_subln": gain((L, 2 * DA_HEAD_DIM)),
        "da_w_o": nrm((L, DA_V_WIDTH, D_MODEL), DA_V_WIDTH ** -0.5),
        "mla_g_cq": gain((L, MLA_Q_LORA)),
        "mla_w_uq": nrm((L, MLA_Q_LORA, MLA_HEADS * (MLA_NOPE + MLA_ROPE)), MLA_Q_LORA ** -0.5),
        "mla_g_ckv": gain((L, MLA_KV_LORA)),
        "mla_w_ukv": nrm((L, MLA_KV_LORA, MLA_HEADS * (MLA_NOPE + MLA_V)), MLA_KV_LORA ** -0.5),
        "mla_w_o": nrm((L, MLA_HEADS * MLA_V, D_MODEL), (MLA_HEADS * MLA_V) ** -0.5),
        "sc_conv": nrm((L, SC_KERNEL, SC_WIDTH), SC_KERNEL ** -0.5),
        "sc_w_o": nrm((L, SC_WIDTH, D_MODEL), SC_WIDTH ** -0.5),
        "cf_conv": nrm((L, CF_KERNEL, CF_WIDTH), CF_KERNEL ** -0.5),
        "cf_conv_b": nrm((L, CF_WIDTH), 0.01),
        "cf_ln_g": gain((L, CF_WIDTH)),
        "cf_ln_b": nrm((L, CF_WIDTH), 0.01),
        "cf_w_o": nrm((L, CF_WIDTH, D_MODEL), CF_WIDTH ** -0.5),
        "w_mix_out": nrm((L, D_MODEL, D_MODEL), D_MODEL ** -0.5),
        "g_post_mix": gain((L, D_MODEL)),
        "g_pre_ffn": gain((L, D_MODEL)),
        "w_up": nrm((L, D_MODEL, 2 * FFN_DIM), D_MODEL ** -0.5),
        "ffn_conv": nrm((L, FFN_KERNEL, FFN_DIM), FFN_KERNEL ** -0.5),
        "w_down": nrm((L, FFN_DIM, D_MODEL), FFN_DIM ** -0.5),
        "g_post_ffn": gain((L, D_MODEL)),
    }


def reference(x, c, positions, w_ada, b_ada, g_pre_mix, w_in,
              da_lam_q1, da_lam_k1, da_lam_q2, da_lam_k2, da_g_subln, da_w_o,
              mla_g_cq, mla_w_uq, mla_g_ckv, mla_w_ukv, mla_w_o,
              sc_conv, sc_w_o,
              cf_conv, cf_conv_b, cf_ln_g, cf_ln_b, cf_w_o,
              w_mix_out, g_post_mix, g_pre_ffn, w_up, ffn_conv, w_down, g_post_ffn):
    cos, sin = rope_tables(positions)
    c_act = jax.nn.silu(c)
    for l in range(DEPTH):
        mod = (c_act @ w_ada[l] + b_ada[l])[:, None, :]
        shift1, scale1, gate1, shift2, scale2, gate2 = jnp.split(mod, N_MOD, axis=-1)
        x = x + gate1 * token_mixing(
            x, shift1, scale1, cos, sin, l, g_pre_mix[l], w_in[l],
            da_lam_q1[l], da_lam_k1[l], da_lam_q2[l], da_lam_k2[l], da_g_subln[l], da_w_o[l],
            mla_g_cq[l], mla_w_uq[l], mla_g_ckv[l], mla_w_ukv[l], mla_w_o[l],
            sc_conv[l], sc_w_o[l],
            cf_conv[l], cf_conv_b[l], cf_ln_g[l], cf_ln_b[l], cf_w_o[l],
            w_mix_out[l], g_post_mix[l])
        x = x + gate2 * channel_mixing(
            x, shift2, scale2, g_pre_ffn[l], w_up[l], ffn_conv[l], w_down[l], g_post_ffn[l])
    return x
```

```python
import functools
import math

import jax
import jax.numpy as jnp
from jax import lax
from jax.experimental import pallas as pl
from jax.experimental.pallas import tpu as pltpu

D_MODEL = 1024
DA_HEADS = 4
DA_HEAD_DIM = 64
DA_WIDTH = DA_HEADS * 2 * DA_HEAD_DIM
MLA_HEADS = 8
MLA_Q_LORA = 384
MLA_KV_LORA = 256
MLA_NOPE = 64
MLA_ROPE = 32
MLA_V = 64
ROPE_THETA = 10000.0
SC_WIDTH = 512
SC_KERNEL = 3
CF_WIDTH = 512
CF_KERNEL = 31
N_BRANCHES = 4
FFN_DIM = 2816
FFN_KERNEL = 3
N_MOD = 6
NORM_EPS = 1e-6
LN_EPS = 1e-5

HEAD_PAD = 128
LOG2E = math.log2(math.e)
NEG_BIG = -1e30

VMEM_LIMIT_BYTES = 56 * 1024 * 1024
TOKEN_TILE = 512
ATTN_TILE = 256
CF_ROW_CHUNK = 32
FFN_COL_CHUNK = 256
SC_HALO = 8
CF_HALO = 32
FFN_HALO = 8

BF16 = jnp.bfloat16
F32 = jnp.float32


def _dot(a, b):
    return jnp.dot(a, b, preferred_element_type=F32)


def _dot_nt(a, b):
    return lax.dot_general(a, b, (((1,), (1,)), ((), ())), preferred_element_type=F32)


def _sigmoid(x):
    return 1.0 / (1.0 + jnp.exp(-x))


def _rmsnorm(x, g):
    return x * lax.rsqrt(jnp.mean(x * x, axis=-1, keepdims=True) + NORM_EPS) * g


def _const_spec(shape):
    nd = len(shape)
    return pl.BlockSpec(shape, lambda *_: (0,) * nd, pipeline_mode=pl.Buffered(1))


def _params(*sem):
    return pltpu.CompilerParams(dimension_semantics=sem, vmem_limit_bytes=VMEM_LIMIT_BYTES)


def _ada_kernel(c_ref, w_ref, b_ref, o_ref):
    c = c_ref[...]
    act = (c * _sigmoid(c)).astype(BF16)
    o_ref[0] = _dot(act, w_ref[0].astype(BF16)) + b_ref[0]


def _ada_modulation(c, w_ada, b_ada):
    depth, d, n = w_ada.shape
    bsz = c.shape[0]
    out = pl.pallas_call(
        _ada_kernel,
        out_shape=jax.ShapeDtypeStruct((depth, bsz, n), F32),
        grid=(depth, n // d),
        in_specs=[
            pl.BlockSpec((bsz, d), lambda l, j: (0, 0)),
            pl.BlockSpec((1, d, d), lambda l, j: (l, 0, j)),
            pl.BlockSpec((1, 1, d), lambda l, j: (l, 0, j)),
        ],
        out_specs=pl.BlockSpec((1, bsz, d), lambda l, j: (l, 0, j)),
        compiler_params=_params("arbitrary", "arbitrary"),
        name="ada_modulation",
    )(c, w_ada, b_ada.reshape(depth, 1, n))
    return out.reshape(depth, bsz, N_MOD, d)


_C_SC = 0
_C_CF = _C_SC + 3 * SC_WIDTH
_C_DAK = _C_CF + 2 * CF_WIDTH
_C_CQ = _C_DAK + DA_WIDTH
_C_END = _C_CQ + MLA_Q_LORA + MLA_KV_LORA + 2 * HEAD_PAD


def _in_proj_kernel(x_ref, mod_ref, gpre_ref, pos_ref, invf_ref, wnat_ref, wt_ref,
                    gcq_ref, wuqt_ref, wuqswt_ref, gckv_ref, wukvk_ref, wukvvt_ref,
                    scconv_ref, cfconv_ref, cfb_ref, lng_ref, lnb_ref,
                    qtda_ref, kda_ref, vtda_ref, qtm_ref, km_ref, vtm_ref, yc_ref, yd_ref,
                    wbuf, ubuf, *, tm, tq):
    t = pl.program_id(1)

    @pl.when(t == 0)
    def _():
        wbuf[0:SC_HALO, :] = jnp.zeros((SC_HALO, SC_WIDTH), F32)
        ubuf[0:CF_HALO, :] = jnp.zeros((CF_HALO, CF_WIDTH), F32)

    x = x_ref[0]
    shift = mod_ref[0, 0:1, :]
    scale = mod_ref[0, 1:2, :]
    h = (_rmsnorm(x, gpre_ref[...]) * (1.0 + scale) + shift).astype(BF16)

    sc = _dot(h, wnat_ref[:, _C_SC:_C_CF])
    w = sc[:, 2 * SC_WIDTH:3 * SC_WIDTH] * sc[:, 0:SC_WIDTH]
    wbuf[SC_HALO:SC_HALO + tm, :] = w
    conv = (scconv_ref[2:3, :] * w
            + scconv_ref[1:2, :] * wbuf[SC_HALO - 1:SC_HALO - 1 + tm, :]
            + scconv_ref[0:1, :] * wbuf[SC_HALO - 2:SC_HALO - 2 + tm, :])
    yc_ref[0] = (sc[:, SC_WIDTH:2 * SC_WIDTH] * conv).astype(BF16)
    wbuf[0:SC_HALO, :] = wbuf[tm:tm + SC_HALO, :]

    cf = _dot(h, wnat_ref[:, _C_CF:_C_DAK])
    ubuf[CF_HALO:CF_HALO + tm, :] = cf[:, 0:CF_WIDTH] * _sigmoid(cf[:, CF_WIDTH:2 * CF_WIDTH])
    base = CF_HALO - (CF_KERNEL - 1)
    for r0 in range(0, tm, CF_ROW_CHUNK):
        acc = jnp.broadcast_to(cfb_ref[...], (CF_ROW_CHUNK, CF_WIDTH))
        for j in range(CF_KERNEL):
            acc = acc + cfconv_ref[j:j + 1, :] * ubuf[base + r0 + j:base + r0 + j + CF_ROW_CHUNK, :]
        mu = jnp.mean(acc, axis=-1, keepdims=True)
        cen = acc - mu
        var = jnp.mean(cen * cen, axis=-1, keepdims=True)
        y = cen * lax.rsqrt(var + LN_EPS) * lng_ref[...] + lnb_ref[...]
        yd_ref[0, r0:r0 + CF_ROW_CHUNK, :] = (y * _sigmoid(y)).astype(BF16)
    ubuf[0:CF_HALO, :] = ubuf[tm:tm + CF_HALO, :]

    kda_ref[0] = _dot(h, wnat_ref[:, _C_DAK:_C_CQ]).astype(BF16)
    tr = _dot_nt(wt_ref[...], h)
    qs_da = DA_HEAD_DIM ** -0.5 * LOG2E
    for c in range(tm // tq):
        qtda_ref[0, c] = (tr[0:DA_WIDTH, c * tq:(c + 1) * tq] * qs_da).astype(BF16)
        vtda_ref[0, c] = tr[DA_WIDTH:2 * DA_WIDTH, c * tq:(c + 1) * tq].astype(BF16)

    lat = _dot(h, wnat_ref[:, _C_CQ:_C_END])
    o_kv = MLA_Q_LORA
    o_kr = o_kv + MLA_KV_LORA
    cqn = _rmsnorm(lat[:, 0:o_kv], gcq_ref[...]).astype(BF16)
    ckvn = _rmsnorm(lat[:, o_kv:o_kr], gckv_ref[...]).astype(BF16)
    kr_lin = lat[:, o_kr:o_kr + HEAD_PAD]
    kr_swp = lat[:, o_kr + HEAD_PAD:o_kr + 2 * HEAD_PAD]

    half = MLA_ROPE // 2
    ang = invf_ref[...] * pos_ref[0].astype(F32)
    cos_h = jnp.cos(ang)
    sin_h = jnp.sin(ang)
    cos_r = jnp.concatenate([cos_h, cos_h], axis=0)
    sin_r = jnp.concatenate([-sin_h, sin_h], axis=0)

    zeros_lo = jnp.zeros((MLA_NOPE, tm), F32)
    zeros_hi = jnp.zeros((HEAD_PAD - MLA_NOPE - MLA_ROPE, tm), F32)
    cos_n = jnp.concatenate([zeros_lo, cos_r, zeros_hi], axis=0).T
    sin_n = jnp.concatenate([zeros_lo, sin_r, zeros_hi], axis=0).T
    k_rope = kr_lin * cos_n + kr_swp * sin_n
    k_nope = _dot(ckvn, wukvk_ref[...])
    for hd in range(MLA_HEADS):
        km_ref[0, :, hd * HEAD_PAD:(hd + 1) * HEAD_PAD] = (
            k_nope[:, hd * HEAD_PAD:(hd + 1) * HEAD_PAD] + k_rope).astype(BF16)

    vt = _dot_nt(wukvvt_ref[...], ckvn)
    qt_lin = _dot_nt(wuqt_ref[...], cqn)
    qt_swp = _dot_nt(wuqswt_ref[...], cqn)
    qs_mla = (MLA_NOPE + MLA_ROPE) ** -0.5 * LOG2E
    pad_rows = jnp.zeros((HEAD_PAD - MLA_NOPE - MLA_ROPE, tm), F32)
    qt_heads = []
    for hd in range(MLA_HEADS):
        lin = qt_lin[hd * HEAD_PAD:(hd + 1) * HEAD_PAD]
        rot = (lin[MLA_NOPE:MLA_NOPE + MLA_ROPE] * cos_r
               + qt_swp[hd * MLA_ROPE:(hd + 1) * MLA_ROPE] * sin_r)
        qt_heads.append(jnp.concatenate([lin[0:MLA_NOPE], rot, pad_rows], axis=0))
    qt = jnp.concatenate(qt_heads, axis=0) * qs_mla
    for c in range(tm // tq):
        qtm_ref[0, c] = qt[:, c * tq:(c + 1) * tq].astype(BF16)
        vtm_ref[0, c] = vt[:, c * tq:(c + 1) * tq].astype(BF16)


def _in_proj(x, mod, positions3, invf, lw, *, tm, tq):
    bsz, s, d = x.shape
    nt = s // tm
    nq = s // tq
    kern = functools.partial(_in_proj_kernel, tm=tm, tq=tq)
    consts = [lw["g_pre_mix"], None, invf, lw["w_nat"], lw["w_t"], lw["g_cq"], lw["wuq_t"],
              lw["wuq_swp_t"], lw["g_ckv"], lw["wukv_k"], lw["wukv_v_t"], lw["sc_conv"],
              lw["cf_conv"], lw["cf_conv_b"], lw["cf_ln_g"], lw["cf_ln_b"]]
    in_specs = [
        pl.BlockSpec((1, tm, d), lambda b, t: (b, t, 0)),
        pl.BlockSpec((1, N_MOD, d), lambda b, t: (b, 0, 0)),
    ]
    args = [x, mod]
    for a in consts:
        if a is None:
            in_specs.append(pl.BlockSpec((1, 1, tm), lambda b, t: (b, 0, t)))
            args.append(positions3)
        else:
            in_specs.append(_const_spec(a.shape))
            args.append(a)
    slab = tm // tq
    out_shape = (
        jax.ShapeDtypeStruct((bsz, nq, DA_WIDTH, tq), BF16),
        jax.ShapeDtypeStruct((bsz, s, DA_WIDTH), BF16),
        jax.ShapeDtypeStruct((bsz, nq, DA_WIDTH, tq), BF16),
        jax.ShapeDtypeStruct((bsz, nq, MLA_HEADS * HEAD_PAD, tq), BF16),
        jax.ShapeDtypeStruct((bsz, s, MLA_HEADS * HEAD_PAD), BF16),
        jax.ShapeDtypeStruct((bsz, nq, MLA_HEADS * MLA_V, tq), BF16),
        jax.ShapeDtypeStruct((bsz, s, SC_WIDTH), BF16),
        jax.ShapeDtypeStruct((bsz, s, CF_WIDTH), BF16),
    )

    def tspec(rows):
        return pl.BlockSpec((1, slab, rows, tq), lambda b, t: (b, t, 0, 0))

    def nspec(cols):
        return pl.BlockSpec((1, tm, cols), lambda b, t: (b, t, 0))

    out_specs = (tspec(DA_WIDTH), nspec(DA_WIDTH), tspec(DA_WIDTH),
                 tspec(MLA_HEADS * HEAD_PAD), nspec(MLA_HEADS * HEAD_PAD),
                 tspec(MLA_HEADS * MLA_V), nspec(SC_WIDTH), nspec(CF_WIDTH))
    return pl.pallas_call(
        kern,
        out_shape=out_shape,
        grid=(bsz, nt),
        in_specs=in_specs,
        out_specs=out_specs,
        scratch_shapes=[pltpu.VMEM((tm + SC_HALO, SC_WIDTH), F32),
                        pltpu.VMEM((tm + CF_HALO, CF_WIDTH), F32)],
        compiler_params=_params("arbitrary", "arbitrary"),
        name="in_proj",
    )(*args)


def _attn_kernel(*refs, tq, differential, lam_init):
    if differential:
        (qt_ref, k_ref, vt_ref, lq1_ref, lk1_ref, lq2_ref, lk2_ref, gsub_ref,
         o_ref, qcat_ref, m_ref, l_ref, acc_ref) = refs
    else:
        qt_ref, k_ref, vt_ref, o_ref, qcat_ref, m_ref, l_ref, acc_ref = refs
    i = pl.program_id(2)
    tk = tq

    if differential:
        qt = qt_ref[0, 0]
        row = lax.broadcasted_iota(jnp.int32, qt.shape, 0)
        zero = jnp.zeros_like(qt)
        qcat_ref[:, 0:tq] = jnp.where(row < DA_HEAD_DIM, qt, zero)
        qcat_ref[:, tq:2 * tq] = jnp.where(row >= DA_HEAD_DIM, qt, zero)
    m_ref[...] = jnp.full(m_ref.shape, NEG_BIG, F32)
    l_ref[...] = jnp.zeros(l_ref.shape, F32)
    acc_ref[...] = jnp.zeros(acc_ref.shape, F32)

    def tile(j, masked):
        k0 = pl.multiple_of(j * tk, tk)
        if differential:
            s = _dot(k_ref[0, pl.ds(k0, tk), :], qcat_ref[...])
        else:
            s = jnp.concatenate(
                [_dot(k_ref[0, pl.ds(k0, tk), 0:HEAD_PAD], qt_ref[0, 0, 0:HEAD_PAD, :]),
                 _dot(k_ref[0, pl.ds(k0, tk), HEAD_PAD:2 * HEAD_PAD],
                      qt_ref[0, 0, HEAD_PAD:2 * HEAD_PAD, :])], axis=1)
        if masked:
            kpos = lax.broadcasted_iota(jnp.int32, (tk, tq), 0)
            qpos = lax.broadcasted_iota(jnp.int32, (tk, tq), 1)
            keep = kpos <= qpos
            keep = jnp.concatenate([keep, keep], axis=1)
            s = jnp.where(keep, s, NEG_BIG)
        m_old = m_ref[...]
        m_new = jnp.maximum(m_old, jnp.max(s, axis=0, keepdims=True))
        alpha = jnp.exp2(m_old - m_new)
        p = jnp.exp2(s - m_new)
        l_ref[...] = alpha * l_ref[...] + jnp.sum(p, axis=0, keepdims=True)
        m_ref[...] = m_new
        acc_ref[...] = alpha * acc_ref[...] + _dot(vt_ref[0, j], p.astype(BF16))

    def body(j, carry):
        tile(j, False)
        return carry

    lax.fori_loop(0, i, body, 0)
    tile(i, True)

    acc = acc_ref[...]
    l = l_ref[...]
    if differential:
        lam = (jnp.exp(jnp.sum(lq1_ref[...] * lk1_ref[...], axis=-1, keepdims=True))
               - jnp.exp(jnp.sum(lq2_ref[...] * lk2_ref[...], axis=-1, keepdims=True)) + lam_init)
        o = acc[:, 0:tq] / l[:, 0:tq] - lam * (acc[:, tq:2 * tq] / l[:, tq:2 * tq])
        o = o * lax.rsqrt(jnp.mean(o * o, axis=0, keepdims=True) + NORM_EPS)
        o = o * gsub_ref[...] * (1.0 - lam_init)
    else:
        o = jnp.concatenate([acc[0:MLA_V, 0:tq] / l[:, 0:tq],
                             acc[MLA_V:2 * MLA_V, tq:2 * tq] / l[:, tq:2 * tq]], axis=0)
    o_ref[0] = o.T.astype(BF16)


def _attention(qt, k, vt, extra, *, tq, differential, lam_init=0.0):
    bsz, nq, rows, _ = qt.shape
    s = k.shape[1]
    groups = 4
    qrows = rows // groups
    kern = functools.partial(_attn_kernel, tq=tq, differential=differential, lam_init=lam_init)
    in_specs = [
        pl.BlockSpec((1, 1, qrows, tq), lambda b, g, i: (b, i, g, 0)),
        pl.BlockSpec((1, s, qrows), lambda b, g, i: (b, 0, g)),
        pl.BlockSpec((1, nq, HEAD_PAD, tq), lambda b, g, i: (b, 0, g, 0)),
    ]
    for a in extra:
        in_specs.append(_const_spec(a.shape))
    return pl.pallas_call(
        kern,
        out_shape=jax.ShapeDtypeStruct((bsz, s, groups * HEAD_PAD), BF16),
        grid=(bsz, groups, nq),
        in_specs=in_specs,
        out_specs=pl.BlockSpec((1, tq, HEAD_PAD), lambda b, g, i: (b, i, g)),
        scratch_shapes=[pltpu.VMEM((HEAD_PAD, 2 * tq), BF16),
                        pltpu.VMEM((1, 2 * tq), F32),
                        pltpu.VMEM((1, 2 * tq), F32),
                        pltpu.VMEM((HEAD_PAD, 2 * tq), F32)],
        compiler_params=_params("arbitrary", "arbitrary", "arbitrary"),
        name="diff_attention" if differential else "latent_attention",
    )(qt, k, vt, *extra)


def _merge_kernel(x_ref, mod_ref, gpre_ref, wg_ref, ya_ref, yb_ref, yc_ref, yd_ref,
                  woa_ref, wob_ref, woc_ref, wod_ref, wmix_ref, gpost_ref, o_ref):
    x = x_ref[0]
    shift = mod_ref[0, 0:1, :]
    scale = mod_ref[0, 1:2, :]
    gate = mod_ref[0, 2:3, :]
    h = (_rmsnorm(x, gpre_ref[...]) * (1.0 + scale) + shift).astype(BF16)
    merged = None
    branches = ((ya_ref, woa_ref), (yb_ref, wob_ref), (yc_ref, woc_ref), (yd_ref, wod_ref))
    for n, (pre_ref, wo_ref) in enumerate(branches):
        g = _sigmoid(_dot(h, wg_ref[:, n * D_MODEL:(n + 1) * D_MODEL]))
        term = g * _dot(pre_ref[0], wo_ref[...])
        merged = term if merged is None else merged + term
    z = _dot(merged.astype(BF16), wmix_ref[...])
    o_ref[0] = x + gate * _rmsnorm(z, gpost_ref[...])


def _merge(x, mod, ya, yb, yc, yd, lw, *, tm):
    bsz, s, d = x.shape

    def tok(cols):
        return pl.BlockSpec((1, tm, cols), lambda b, t: (b, t, 0))

    consts_a = [lw["g_pre_mix"], lw["w_gates"]]
    consts_b = [lw["da_w_o"], lw["mla_w_o"], lw["sc_w_o"], lw["cf_w_o"], lw["w_mix_out"],
                lw["g_post_mix"]]
    in_specs = ([tok(d), pl.BlockSpec((1, N_MOD, d), lambda b, t: (b, 0, 0))]
                + [_const_spec(a.shape) for a in consts_a]
                + [tok(a.shape[-1]) for a in (ya, yb, yc, yd)]
                + [_const_spec(a.shape) for a in consts_b])
    return pl.pallas_call(
        _merge_kernel,
        out_shape=jax.ShapeDtypeStruct(x.shape, F32),
        grid=(bsz, s // tm),
        in_specs=in_specs,
        out_specs=tok(d),
        compiler_params=_params("arbitrary", "arbitrary"),
        name="merge_out_proj",
    )(x, mod, *consts_a, ya, yb, yc, yd, *consts_b)


def _ffn_kernel(x_ref, mod_ref, gpre_ref, wup_ref, conv_ref, wdown_ref, gpost_ref, o_ref,
                abuf, carry, act_ref, *, tm):
    t = pl.program_id(1)

    @pl.when(t == 0)
    def _():
        carry[...] = jnp.zeros(carry.shape, F32)

    x = x_ref[0]
    shift = mod_ref[0, 3:4, :]
    scale = mod_ref[0, 4:5, :]
    gate = mod_ref[0, 5:6, :]
    h = (_rmsnorm(x, gpre_ref[...]) * (1.0 + scale) + shift).astype(BF16)
    fc = FFN_COL_CHUNK
    for c0 in range(0, FFN_DIM, fc):
        a = _dot(h, wup_ref[:, c0:c0 + fc])
        b = _dot(h, wup_ref[:, FFN_DIM + c0:FFN_DIM + c0 + fc])
        abuf[0:FFN_HALO, :] = carry[:, c0:c0 + fc]
        abuf[FFN_HALO:FFN_HALO + tm, :] = a
        carry[:, c0:c0 + fc] = a[tm - FFN_HALO:tm, :]
        cv = (conv_ref[2:3, c0:c0 + fc] * a
              + conv_ref[1:2, c0:c0 + fc] * abuf[FFN_HALO - 1:FFN_HALO - 1 + tm, :]
              + conv_ref[0:1, c0:c0 + fc] * abuf[FFN_HALO - 2:FFN_HALO - 2 + tm, :])
        act_ref[:, c0:c0 + fc] = (cv * _sigmoid(cv) * b).astype(BF16)
    y = _dot(act_ref[...], wdown_ref[...])
    o_ref[0] = x + gate * _rmsnorm(y, gpost_ref[...])


def _ffn(x, mod, lw, *, tm):
    bsz, s, d = x.shape
    tok = pl.BlockSpec((1, tm, d), lambda b, t: (b, t, 0))
    consts = [lw["g_pre_ffn"], lw["w_up"], lw["ffn_conv"], lw["w_down"], lw["g_post_ffn"]]
    return pl.pallas_call(
        functools.partial(_ffn_kernel, tm=tm),
        out_shape=jax.ShapeDtypeStruct(x.shape, F32),
        grid=(bsz, s // tm),
        in_specs=[tok, pl.BlockSpec((1, N_MOD, d), lambda b, t: (b, 0, 0))]
        + [_const_spec(a.shape) for a in consts],
        out_specs=tok,
        scratch_shapes=[pltpu.VMEM((tm + FFN_HALO, FFN_COL_CHUNK), F32),
                        pltpu.VMEM((FFN_HALO, FFN_DIM), F32),
                        pltpu.VMEM((tm, FFN_DIM), BF16)],
        compiler_params=_params("arbitrary", "arbitrary"),
        name="gated_mlp",
    )(x, mod, *consts)


def _split_w_in(w_in):
    sizes = (DA_WIDTH, DA_WIDTH, DA_WIDTH, MLA_Q_LORA, MLA_KV_LORA, MLA_ROPE,
             SC_WIDTH, SC_WIDTH, SC_WIDTH, 2 * CF_WIDTH, N_BRANCHES * D_MODEL)
    out, acc = [], 0
    for sz in sizes:
        out.append(w_in[:, acc:acc + sz])
        acc += sz
    return out


def _prep_layer(p, l):
    row = lambda a: a[l].reshape(1, -1)
    (w_q, w_k, w_v, w_cq, w_ckv, w_kr, w_u, w_b, w_c, w_cf, w_g) = _split_w_in(p["w_in"][l])
    d = w_q.shape[0]
    half = MLA_ROPE // 2
    pad_hi = HEAD_PAD - MLA_NOPE - MLA_ROPE
    z = lambda n: jnp.zeros((d, n), F32)
    kr_lin = jnp.concatenate([z(MLA_NOPE), w_kr, z(pad_hi)], axis=1)
    kr_swp = jnp.concatenate([z(MLA_NOPE), w_kr[:, half:], w_kr[:, :half], z(pad_hi)], axis=1)
    w_nat = jnp.concatenate([w_u, w_b, w_c, w_cf, w_k, w_cq, w_ckv, kr_lin, kr_swp], axis=1)
    w_t = jnp.concatenate([w_q, w_v], axis=1).T

    w_uq = p["mla_w_uq"][l].reshape(MLA_Q_LORA, MLA_HEADS, MLA_NOPE + MLA_ROPE)
    uq_pad = jnp.pad(w_uq, ((0, 0), (0, 0), (0, pad_hi)))
    rope = w_uq[:, :, MLA_NOPE:]
    uq_swp = jnp.concatenate([rope[:, :, half:], rope[:, :, :half]], axis=2)
    w_ukv = p["mla_w_ukv"][l].reshape(MLA_KV_LORA, MLA_HEADS, MLA_NOPE + MLA_V)
    ukv_k = jnp.pad(w_ukv[:, :, :MLA_NOPE], ((0, 0), (0, 0), (0, HEAD_PAD - MLA_NOPE)))
    ukv_v = w_ukv[:, :, MLA_NOPE:]
    return {
        "g_pre_mix": row(p["g_pre_mix"]),
        "w_nat": w_nat.astype(BF16),
        "w_t": w_t.astype(BF16),
        "w_gates": w_g.astype(BF16),
        "g_cq": row(p["mla_g_cq"]),
        "wuq_t": uq_pad.reshape(MLA_Q_LORA, -1).T.astype(BF16),
        "wuq_swp_t": uq_swp.reshape(MLA_Q_LORA, -1).T.astype(BF16),
        "g_ckv": row(p["mla_g_ckv"]),
        "wukv_k": ukv_k.reshape(MLA_KV_LORA, -1).astype(BF16),
        "wukv_v_t": ukv_v.reshape(MLA_KV_LORA, -1).T.astype(BF16),
        "sc_conv": p["sc_conv"][l],
        "cf_conv": p["cf_conv"][l],
        "cf_conv_b": row(p["cf_conv_b"]),
        "cf_ln_g": row(p["cf_ln_g"]),
        "cf_ln_b": row(p["cf_ln_b"]),
        "lam": [row(p[n]) for n in ("da_lam_q1", "da_lam_k1", "da_lam_q2", "da_lam_k2")],
        "g_subln": p["da_g_subln"][l].reshape(-1, 1),
        "da_w_o": p["da_w_o"][l].astype(BF16),
        "mla_w_o": p["mla_w_o"][l].astype(BF16),
        "sc_w_o": p["sc_w_o"][l].astype(BF16),
        "cf_w_o": p["cf_w_o"][l].astype(BF16),
        "w_mix_out": p["w_mix_out"][l].astype(BF16),
        "g_post_mix": row(p["g_post_mix"]),
        "g_pre_ffn": row(p["g_pre_ffn"]),
        "w_up": p["w_up"][l].astype(BF16),
        "ffn_conv": p["ffn_conv"][l],
        "w_down": p["w_down"][l].astype(BF16),
        "g_post_ffn": row(p["g_post_ffn"]),
    }


def kernel(x, c, positions, w_ada, b_ada, g_pre_mix, w_in, da_lam_q1, da_lam_k1, da_lam_q2, da_lam_k2, da_g_subln, da_w_o, mla_g_cq, mla_w_uq, mla_g_ckv, mla_w_ukv, mla_w_o, sc_conv, sc_w_o, cf_conv, cf_conv_b, cf_ln_g, cf_ln_b, cf_w_o, w_mix_out, g_post_mix, g_pre_ffn, w_up, ffn_conv, w_down, g_post_ffn):
    p = dict(g_pre_mix=g_pre_mix, w_in=w_in, da_lam_q1=da_lam_q1, da_lam_k1=da_lam_k1,
             da_lam_q2=da_lam_q2, da_lam_k2=da_lam_k2, da_g_subln=da_g_subln, da_w_o=da_w_o,
             mla_g_cq=mla_g_cq, mla_w_uq=mla_w_uq, mla_g_ckv=mla_g_ckv, mla_w_ukv=mla_w_ukv,
             mla_w_o=mla_w_o, sc_conv=sc_conv, sc_w_o=sc_w_o, cf_conv=cf_conv,
             cf_conv_b=cf_conv_b, cf_ln_g=cf_ln_g, cf_ln_b=cf_ln_b, cf_w_o=cf_w_o,
             w_mix_out=w_mix_out, g_post_mix=g_post_mix, g_pre_ffn=g_pre_ffn, w_up=w_up,
             ffn_conv=ffn_conv, w_down=w_down, g_post_ffn=g_post_ffn)
    bsz, s, _ = x.shape
    depth = w_ada.shape[0]
    tm = min(TOKEN_TILE, s)
    tq = min(ATTN_TILE, s)
    assert s % tm == 0 and tm % tq == 0

    half = MLA_ROPE // 2
    inv_freq = ROPE_THETA ** (-jnp.arange(half, dtype=F32) / half)
    invf = jnp.broadcast_to(inv_freq[:, None], (half, tm))
    positions3 = positions.reshape(bsz, 1, s)

    mod = _ada_modulation(c, w_ada, b_ada)
    for l in range(depth):
        lw = _prep_layer(p, l)
        lam_init = 0.8 - 0.6 * math.exp(-0.3 * l)
        qt_da, k_da, vt_da, qt_m, k_m, vt_m, yc, yd = _in_proj(
            x, mod[l], positions3, invf, lw, tm=tm, tq=tq)
        ya = _attention(qt_da, k_da, vt_da, lw["lam"] + [lw["g_subln"]],
                        tq=tq, differential=True, lam_init=lam_init)
        yb = _attention(qt_m, k_m, vt_m, [], tq=tq, differential=False)
        x = _merge(x, mod[l], ya, yb, yc, yd, lw, tm=tm)
        x = _ffn(x, mod[l], lw, tm=tm)
    return x
```

```python
import functools
import math

import jax
import jax.numpy as jnp
from jax import lax
from jax.experimental import pallas as pl
from jax.experimental.pallas import tpu as pltpu

D_MODEL = 1024
DA_HEADS = 4
DA_HEAD_DIM = 64
DA_WIDTH = DA_HEADS * 2 * DA_HEAD_DIM
MLA_HEADS = 8
MLA_Q_LORA = 384
MLA_KV_LORA = 256
MLA_NOPE = 64
MLA_ROPE = 32
MLA_V = 64
ROPE_THETA = 10000.0
SC_WIDTH = 512
SC_KERNEL = 3
CF_WIDTH = 512
CF_KERNEL = 31
N_BRANCHES = 4
FFN_DIM = 2816
FFN_KERNEL = 3
N_MOD = 6
NORM_EPS = 1e-6
LN_EPS = 1e-5

HEAD_PAD = 128
LOG2E = math.log2(math.e)
NEG_BIG = -1e30

VMEM_LIMIT_BYTES = 56 * 1024 * 1024
TOKEN_TILE = 512
ATTN_TILE = 256
CF_ROW_CHUNK = 32
FFN_COL_CHUNK = 256
SC_HALO = 8
CF_HALO = 32
FFN_HALO = 8
ATTN_GROUPS = 4

BF16 = jnp.bfloat16
F32 = jnp.float32


def _dot(a, b):
    return jnp.dot(a, b, preferred_element_type=F32)


def _dot_nt(a, b):
    return lax.dot_general(a, b, (((1,), (1,)), ((), ())), preferred_element_type=F32)


def _sigmoid(x):
    return 1.0 / (1.0 + jnp.exp(-x))


def _rmsnorm(x, g):
    return x * lax.rsqrt(jnp.mean(x * x, axis=-1, keepdims=True) + NORM_EPS) * g


def _const_spec(shape):
    nd = len(shape)
    return pl.BlockSpec(shape, lambda *_: (0,) * nd, pipeline_mode=pl.Buffered(1))


def _params(*sem):
    return pltpu.CompilerParams(dimension_semantics=sem, vmem_limit_bytes=VMEM_LIMIT_BYTES)


def _ada_kernel(c_ref, w_ref, b_ref, o_ref):
    c = c_ref[...]
    act = (c * _sigmoid(c)).astype(BF16)
    o_ref[0] = _dot(act, w_ref[0].astype(BF16)) + b_ref[0]


def _ada_modulation(c, w_ada, b_ada):
    depth, d, n = w_ada.shape
    bsz = c.shape[0]
    out = pl.pallas_call(
        _ada_kernel,
        out_shape=jax.ShapeDtypeStruct((depth, bsz, n), F32),
        grid=(depth, n // d),
        in_specs=[
            pl.BlockSpec((bsz, d), lambda l, j: (0, 0)),
            pl.BlockSpec((1, d, d), lambda l, j: (l, 0, j)),
            pl.BlockSpec((1, 1, d), lambda l, j: (l, 0, j)),
        ],
        out_specs=pl.BlockSpec((1, bsz, d), lambda l, j: (l, 0, j)),
        compiler_params=_params("arbitrary", "arbitrary"),
        name="ada_modulation",
    )(c, w_ada, b_ada.reshape(depth, 1, n))
    return out.reshape(depth, bsz, N_MOD, d)


_C_SC = 0
_C_CF = _C_SC + 3 * SC_WIDTH
_C_DAK = _C_CF + 2 * CF_WIDTH
_C_CQ = _C_DAK + DA_WIDTH
_C_END = _C_CQ + MLA_Q_LORA + MLA_KV_LORA + 2 * HEAD_PAD


def _in_proj_kernel(x_ref, mod_ref, gpre_ref, pos_ref, invf_ref, wnat_ref, wt_ref,
                    gcq_ref, wuqt_ref, wuqswt_ref, gckv_ref, wukvk_ref, wukvvt_ref,
                    scconv_ref, cfconv_ref, cfb_ref, lng_ref, lnb_ref,
                    qtda_ref, kda_ref, vtda_ref, qtm_ref, km_ref, vtm_ref, yc_ref, yd_ref,
                    wbuf, ubuf, *, tm, tq):
    t = pl.program_id(1)

    @pl.when(t == 0)
    def _():
        wbuf[0:SC_HALO, :] = jnp.zeros((SC_HALO, SC_WIDTH), F32)
        ubuf[0:CF_HALO, :] = jnp.zeros((CF_HALO, CF_WIDTH), F32)

    x = x_ref[0]
    shift = mod_ref[0, 0:1, :]
    scale = mod_ref[0, 1:2, :]
    h = (_rmsnorm(x, gpre_ref[...]) * (1.0 + scale) + shift).astype(BF16)

    sc = _dot(h, wnat_ref[:, _C_SC:_C_CF])
    w = sc[:, 2 * SC_WIDTH:3 * SC_WIDTH] * sc[:, 0:SC_WIDTH]
    wbuf[SC_HALO:SC_HALO + tm, :] = w
    conv = (scconv_ref[2:3, :] * w
            + scconv_ref[1:2, :] * wbuf[SC_HALO - 1:SC_HALO - 1 + tm, :]
            + scconv_ref[0:1, :] * wbuf[SC_HALO - 2:SC_HALO - 2 + tm, :])
    yc_ref[0] = (sc[:, SC_WIDTH:2 * SC_WIDTH] * conv).astype(BF16)
    wbuf[0:SC_HALO, :] = wbuf[tm:tm + SC_HALO, :]

    cf = _dot(h, wnat_ref[:, _C_CF:_C_DAK])
    ubuf[CF_HALO:CF_HALO + tm, :] = cf[:, 0:CF_WIDTH] * _sigmoid(cf[:, CF_WIDTH:2 * CF_WIDTH])
    base = CF_HALO - (CF_KERNEL - 1)
    for r0 in range(0, tm, CF_ROW_CHUNK):
        acc = jnp.broadcast_to(cfb_ref[...], (CF_ROW_CHUNK, CF_WIDTH))
        for j in range(CF_KERNEL):
            acc = acc + cfconv_ref[j:j + 1, :] * ubuf[base + r0 + j:base + r0 + j + CF_ROW_CHUNK, :]
        mu = jnp.mean(acc, axis=-1, keepdims=True)
        cen = acc - mu
        var = jnp.mean(cen * cen, axis=-1, keepdims=True)
        y = cen * lax.rsqrt(var + LN_EPS) * lng_ref[...] + lnb_ref[...]
        yd_ref[0, r0:r0 + CF_ROW_CHUNK, :] = (y * _sigmoid(y)).astype(BF16)
    ubuf[0:CF_HALO, :] = ubuf[tm:tm + CF_HALO, :]

    kda_ref[0] = _dot(h, wnat_ref[:, _C_DAK:_C_CQ]).astype(BF16)
    tr = _dot_nt(wt_ref[...], h)
    qs_da = DA_HEAD_DIM ** -0.5 * LOG2E
    for c in range(tm // tq):
        qtda_ref[0, c] = (tr[0:DA_WIDTH, c * tq:(c + 1) * tq] * qs_da).astype(BF16)
        vtda_ref[0, c] = tr[DA_WIDTH:2 * DA_WIDTH, c * tq:(c + 1) * tq].astype(BF16)

    lat = _dot(h, wnat_ref[:, _C_CQ:_C_END])
    o_kv = MLA_Q_LORA
    o_kr = o_kv + MLA_KV_LORA
    cqn = _rmsnorm(lat[:, 0:o_kv], gcq_ref[...]).astype(BF16)
    ckvn = _rmsnorm(lat[:, o_kv:o_kr], gckv_ref[...]).astype(BF16)
    kr_lin = lat[:, o_kr:o_kr + HEAD_PAD]
    kr_swp = lat[:, o_kr + HEAD_PAD:o_kr + 2 * HEAD_PAD]

    half = MLA_ROPE // 2
    ang = invf_ref[...] * pos_ref[0].astype(F32)
    cos_h = jnp.cos(ang)
    sin_h = jnp.sin(ang)
    cos_r = jnp.concatenate([cos_h, cos_h], axis=0)
    sin_r = jnp.concatenate([-sin_h, sin_h], axis=0)

    zeros_lo = jnp.zeros((MLA_NOPE, tm), F32)
    zeros_hi = jnp.zeros((HEAD_PAD - MLA_NOPE - MLA_ROPE, tm), F32)
    cos_n = jnp.concatenate([zeros_lo, cos_r, zeros_hi], axis=0).T
    sin_n = jnp.concatenate([zeros_lo, sin_r, zeros_hi], axis=0).T
    k_rope = kr_lin * cos_n + kr_swp * sin_n
    k_nope = _dot(ckvn, wukvk_ref[...])
    for hd in range(MLA_HEADS):
        km_ref[0, :, hd * HEAD_PAD:(hd + 1) * HEAD_PAD] = (
            k_nope[:, hd * HEAD_PAD:(hd + 1) * HEAD_PAD] + k_rope).astype(BF16)

    vt = _dot_nt(wukvvt_ref[...], ckvn)
    qt_lin = _dot_nt(wuqt_ref[...], cqn)
    qt_swp = _dot_nt(wuqswt_ref[...], cqn)
    qs_mla = (MLA_NOPE + MLA_ROPE) ** -0.5 * LOG2E
    pad_rows = jnp.zeros((HEAD_PAD - MLA_NOPE - MLA_ROPE, tm), F32)
    qt_heads = []
    for hd in range(MLA_HEADS):
        lin = qt_lin[hd * HEAD_PAD:(hd + 1) * HEAD_PAD]
        rot = (lin[MLA_NOPE:MLA_NOPE + MLA_ROPE] * cos_r
               + qt_swp[hd * MLA_ROPE:(hd + 1) * MLA_ROPE] * sin_r)
        qt_heads.append(jnp.concatenate([lin[0:MLA_NOPE], rot, pad_rows], axis=0))
    qt = jnp.concatenate(qt_heads, axis=0) * qs_mla
    for c in range(tm // tq):
        qtm_ref[0, c] = qt[:, c * tq:(c + 1) * tq].astype(BF16)
        vtm_ref[0, c] = vt[:, c * tq:(c + 1) * tq].astype(BF16)


def _in_proj(x, mod, positions3, invf, lw, *, tm, tq):
    bsz, s, d = x.shape
    nt = s // tm
    nq = s // tq
    kern = functools.partial(_in_proj_kernel, tm=tm, tq=tq)
    consts = [lw["g_pre_mix"], None, invf, lw["w_nat"], lw["w_t"], lw["g_cq"], lw["wuq_t"],
              lw["wuq_swp_t"], lw["g_ckv"], lw["wukv_k"], lw["wukv_v_t"], lw["sc_conv"],
              lw["cf_conv"], lw["cf_conv_b"], lw["cf_ln_g"], lw["cf_ln_b"]]
    in_specs = [
        pl.BlockSpec((1, tm, d), lambda b, t: (b, t, 0)),
        pl.BlockSpec((1, N_MOD, d), lambda b, t: (b, 0, 0)),
    ]
    args = [x, mod]
    for a in consts:
        if a is None:
            in_specs.append(pl.BlockSpec((1, 1, tm), lambda b, t: (b, 0, t)))
            args.append(positions3)
        else:
            in_specs.append(_const_spec(a.shape))
            args.append(a)
    slab = tm // tq
    out_shape = (
        jax.ShapeDtypeStruct((bsz, nq, DA_WIDTH, tq), BF16),
        jax.ShapeDtypeStruct((bsz, s, DA_WIDTH), BF16),
        jax.ShapeDtypeStruct((bsz, nq, DA_WIDTH, tq), BF16),
        jax.ShapeDtypeStruct((bsz, nq, MLA_HEADS * HEAD_PAD, tq), BF16),
        jax.ShapeDtypeStruct((bsz, s, MLA_HEADS * HEAD_PAD), BF16),
        jax.ShapeDtypeStruct((bsz, nq, MLA_HEADS * MLA_V, tq), BF16),
        jax.ShapeDtypeStruct((bsz, s, SC_WIDTH), BF16),
        jax.ShapeDtypeStruct((bsz, s, CF_WIDTH), BF16),
    )

    def tspec(rows):
        return pl.BlockSpec((1, slab, rows, tq), lambda b, t: (b, t, 0, 0))

    def nspec(cols):
        return pl.BlockSpec((1, tm, cols), lambda b, t: (b, t, 0))

    out_specs = (tspec(DA_WIDTH), nspec(DA_WIDTH), tspec(DA_WIDTH),
                 tspec(MLA_HEADS * HEAD_PAD), nspec(MLA_HEADS * HEAD_PAD),
                 tspec(MLA_HEADS * MLA_V), nspec(SC_WIDTH), nspec(CF_WIDTH))
    return pl.pallas_call(
        kern,
        out_shape=out_shape,
        grid=(bsz, nt),
        in_specs=in_specs,
        out_specs=out_specs,
        scratch_shapes=[pltpu.VMEM((tm + SC_HALO, SC_WIDTH), F32),
                        pltpu.VMEM((tm + CF_HALO, CF_WIDTH), F32)],
        compiler_params=_params("arbitrary", "arbitrary"),
        name="in_proj",
    )(*args)


def _attn_kernel(*refs, tq, differential, lam_init):
    if differential:
        (qt_ref, k_ref, vt_ref, lq1_ref, lk1_ref, lq2_ref, lk2_ref, gsub_ref,
         o_ref, qcat_ref, m_ref, l_ref, acc_ref, s_ref, smax_ref) = refs
    else:
        qt_ref, k_ref, vt_ref, o_ref, qcat_ref, m_ref, l_ref, acc_ref, s_ref, smax_ref = refs
    i = pl.program_id(1)
    tk = tq
    qrows = qt_ref.shape[2] // ATTN_GROUPS

    for g in range(ATTN_GROUPS):
        if differential:
            qt = qt_ref[0, 0, g * qrows:(g + 1) * qrows, :]
            row = lax.broadcasted_iota(jnp.int32, qt.shape, 0)
            zero = jnp.zeros_like(qt)
            qcat_ref[g, :, 0:tq] = jnp.where(row < DA_HEAD_DIM, qt, zero)
            qcat_ref[g, :, tq:2 * tq] = jnp.where(row >= DA_HEAD_DIM, qt, zero)
        m_ref[g] = jnp.full(m_ref.shape[1:], NEG_BIG, F32)
        l_ref[g] = jnp.zeros(l_ref.shape[1:], F32)
        acc_ref[g] = jnp.zeros(acc_ref.shape[1:], F32)

    def scores(j, slot):
        k0 = pl.multiple_of(j * tk, tk)
        for g in range(ATTN_GROUPS):
            c0 = g * qrows
            if differential:
                s = _dot(k_ref[0, pl.ds(k0, tk), c0:c0 + qrows], qcat_ref[g])
                s_ref[slot, g] = s
                smax_ref[slot, g] = jnp.max(s, axis=0, keepdims=True)
            else:
                for n in range(2):
                    r0 = c0 + n * HEAD_PAD
                    s = _dot(k_ref[0, pl.ds(k0, tk), r0:r0 + HEAD_PAD],
                             qt_ref[0, 0, r0:r0 + HEAD_PAD, :])
                    s_ref[slot, g, :, n * tq:(n + 1) * tq] = s
                    smax_ref[slot, g, :, n * tq:(n + 1) * tq] = jnp.max(s, axis=0, keepdims=True)

    def softmax_pv(j, slot, masked):
        for g in range(ATTN_GROUPS):
            s = s_ref[slot, g]
            if masked:
                kpos = lax.broadcasted_iota(jnp.int32, (tk, tq), 0)
                qpos = lax.broadcasted_iota(jnp.int32, (tk, tq), 1)
                keep = kpos <= qpos
                keep = jnp.concatenate([keep, keep], axis=1)
                s = jnp.where(keep, s, NEG_BIG)
                smax = jnp.max(s, axis=0, keepdims=True)
            else:
                smax = smax_ref[slot, g]
            m_old = m_ref[g]
            m_new = jnp.maximum(m_old, smax)
            alpha = jnp.exp2(m_old - m_new)
            p = jnp.exp2(s - m_new)
            l_ref[g] = alpha * l_ref[g] + jnp.sum(p, axis=0, keepdims=True)
            m_ref[g] = m_new
            vt = vt_ref[0, j, g * HEAD_PAD:(g + 1) * HEAD_PAD, :]
            acc_ref[g] = alpha * acc_ref[g] + _dot(vt, p.astype(BF16))

    scores(0, 0)

    def body(jj, carry):
        j = 2 * jj
        scores(j + 1, 1)
        softmax_pv(j, 0, False)
        scores(j + 2, 0)
        softmax_pv(j + 1, 1, False)
        return carry

    lax.fori_loop(0, i // 2, body, 0)

    @pl.when(i % 2 == 0)
    def _():
        softmax_pv(i, 0, True)

    @pl.when(i % 2 == 1)
    def _():
        scores(i, 1)
        softmax_pv(i - 1, 0, False)
        softmax_pv(i, 1, True)

    if differential:
        lam = (jnp.exp(jnp.sum(lq1_ref[...] * lk1_ref[...], axis=-1, keepdims=True))
               - jnp.exp(jnp.sum(lq2_ref[...] * lk2_ref[...], axis=-1, keepdims=True)) + lam_init)
    for g in range(ATTN_GROUPS):
        acc = acc_ref[g]
        l = l_ref[g]
        if differential:
            o = acc[:, 0:tq] / l[:, 0:tq] - lam * (acc[:, tq:2 * tq] / l[:, tq:2 * tq])
            o = o * lax.rsqrt(jnp.mean(o * o, axis=0, keepdims=True) + NORM_EPS)
            o = o * gsub_ref[...] * (1.0 - lam_init)
        else:
            o = jnp.concatenate([acc[0:MLA_V, 0:tq] / l[:, 0:tq],
                                 acc[MLA_V:2 * MLA_V, tq:2 * tq] / l[:, tq:2 * tq]], axis=0)
        o_ref[0, :, g * HEAD_PAD:(g + 1) * HEAD_PAD] = o.T.astype(BF16)


def _attention(qt, k, vt, extra, *, tq, differential, lam_init=0.0):
    bsz, nq, rows, _ = qt.shape
    s = k.shape[1]
    width = ATTN_GROUPS * HEAD_PAD
    kern = functools.partial(_attn_kernel, tq=tq, differential=differential, lam_init=lam_init)
    in_specs = [
        pl.BlockSpec((1, 1, rows, tq), lambda b, i: (b, i, 0, 0)),
        pl.BlockSpec((1, s, rows), lambda b, i: (b, 0, 0)),
        pl.BlockSpec((1, nq, width, tq), lambda b, i: (b, 0, 0, 0)),
    ]
    for a in extra:
        in_specs.append(_const_spec(a.shape))
    return pl.pallas_call(
        kern,
        out_shape=jax.ShapeDtypeStruct((bsz, s, width), BF16),
        grid=(bsz, nq),
        in_specs=in_specs,
        out_specs=pl.BlockSpec((1, tq, width), lambda b, i: (b, i, 0)),
        scratch_shapes=[pltpu.VMEM((ATTN_GROUPS, HEAD_PAD, 2 * tq), BF16),
                        pltpu.VMEM((ATTN_GROUPS, 1, 2 * tq), F32),
                        pltpu.VMEM((ATTN_GROUPS, 1, 2 * tq), F32),
                        pltpu.VMEM((ATTN_GROUPS, HEAD_PAD, 2 * tq), F32),
                        pltpu.VMEM((2, ATTN_GROUPS, tq, 2 * tq), F32),
                        pltpu.VMEM((2, ATTN_GROUPS, 1, 2 * tq), F32)],
        compiler_params=_params("arbitrary", "arbitrary"),
        name="diff_attention" if differential else "latent_attention",
    )(qt, k, vt, *extra)


def _merge_kernel(x_ref, mod_ref, gpre_ref, wg_ref, ya_ref, yb_ref, yc_ref, yd_ref,
                  woa_ref, wob_ref, woc_ref, wod_ref, wmix_ref, gpost_ref, o_ref):
    x = x_ref[0]
    shift = mod_ref[0, 0:1, :]
    scale = mod_ref[0, 1:2, :]
    gate = mod_ref[0, 2:3, :]
    h = (_rmsnorm(x, gpre_ref[...]) * (1.0 + scale) + shift).astype(BF16)
    merged = None
    branches = ((ya_ref, woa_ref), (yb_ref, wob_ref), (yc_ref, woc_ref), (yd_ref, wod_ref))
    for n, (pre_ref, wo_ref) in enumerate(branches):
        g = _sigmoid(_dot(h, wg_ref[:, n * D_MODEL:(n + 1) * D_MODEL]))
        term = g * _dot(pre_ref[0], wo_ref[...])
        merged = term if merged is None else merged + term
    z = _dot(merged.astype(BF16), wmix_ref[...])
    o_ref[0] = x + gate * _rmsnorm(z, gpost_ref[...])


def _merge(x, mod, ya, yb, yc, yd, lw, *, tm):
    bsz, s, d = x.shape

    def tok(cols):
        return pl.BlockSpec((1, tm, cols), lambda b, t: (b, t, 0))

    consts_a = [lw["g_pre_mix"], lw["w_gates"]]
    consts_b = [lw["da_w_o"], lw["mla_w_o"], lw["sc_w_o"], lw["cf_w_o"], lw["w_mix_out"],
                lw["g_post_mix"]]
    in_specs = ([tok(d), pl.BlockSpec((1, N_MOD, d), lambda b, t: (b, 0, 0))]
                + [_const_spec(a.shape) for a in consts_a]
                + [tok(a.shape[-1]) for a in (ya, yb, yc, yd)]
                + [_const_spec(a.shape) for a in consts_b])
    return pl.pallas_call(
        _merge_kernel,
        out_shape=jax.ShapeDtypeStruct(x.shape, F32),
        grid=(bsz, s // tm),
        in_specs=in_specs,
        out_specs=tok(d),
        compiler_params=_params("arbitrary", "arbitrary"),
        name="merge_out_proj",
    )(x, mod, *consts_a, ya, yb, yc, yd, *consts_b)


def _ffn_kernel(x_ref, mod_ref, gpre_ref, wup_ref, conv_ref, wdown_ref, gpost_ref, o_ref,
                abuf, carry, act_ref, *, tm):
    t = pl.program_id(1)

    @pl.when(t == 0)
    def _():
        carry[...] = jnp.zeros(carry.shape, F32)

    x = x_ref[0]
    shift = mod_ref[0, 3:4, :]
    scale = mod_ref[0, 4:5, :]
    gate = mod_ref[0, 5:6, :]
    h = (_rmsnorm(x, gpre_ref[...]) * (1.0 + scale) + shift).astype(BF16)
    fc = FFN_COL_CHUNK
    for c0 in range(0, FFN_DIM, fc):
        a = _dot(h, wup_ref[:, c0:c0 + fc])
        b = _dot(h, wup_ref[:, FFN_DIM + c0:FFN_DIM + c0 + fc])
        abuf[0:FFN_HALO, :] = carry[:, c0:c0 + fc]
        abuf[FFN_HALO:FFN_HALO + tm, :] = a
        carry[:, c0:c0 + fc] = a[tm - FFN_HALO:tm, :]
        cv = (conv_ref[2:3, c0:c0 + fc] * a
              + conv_ref[1:2, c0:c0 + fc] * abuf[FFN_HALO - 1:FFN_HALO - 1 + tm, :]
              + conv_ref[0:1, c0:c0 + fc] * abuf[FFN_HALO - 2:FFN_HALO - 2 + tm, :])
        act_ref[:, c0:c0 + fc] = (cv * _sigmoid(cv) * b).astype(BF16)
    y = _dot(act_ref[...], wdown_ref[...])
    o_ref[0] = x + gate * _rmsnorm(y, gpost_ref[...])


def _ffn(x, mod, lw, *, tm):
    bsz, s, d = x.shape
    tok = pl.BlockSpec((1, tm, d), lambda b, t: (b, t, 0))
    consts = [lw["g_pre_ffn"], lw["w_up"], lw["ffn_conv"], lw["w_down"], lw["g_post_ffn"]]
    return pl.pallas_call(
        functools.partial(_ffn_kernel, tm=tm),
        out_shape=jax.ShapeDtypeStruct(x.shape, F32),
        grid=(bsz, s // tm),
        in_specs=[tok, pl.BlockSpec((1, N_MOD, d), lambda b, t: (b, 0, 0))]
        + [_const_spec(a.shape) for a in consts],
        out_specs=tok,
        scratch_shapes=[pltpu.VMEM((tm + FFN_HALO, FFN_COL_CHUNK), F32),
                        pltpu.VMEM((FFN_HALO, FFN_DIM), F32),
                        pltpu.VMEM((tm, FFN_DIM), BF16)],
        compiler_params=_params("arbitrary", "arbitrary"),
        name="gated_mlp",
    )(x, mod, *consts)


def _split_w_in(w_in):
    sizes = (DA_WIDTH, DA_WIDTH, DA_WIDTH, MLA_Q_LORA, MLA_KV_LORA, MLA_ROPE,
             SC_WIDTH, SC_WIDTH, SC_WIDTH, 2 * CF_WIDTH, N_BRANCHES * D_MODEL)
    out, acc = [], 0
    for sz in sizes:
        out.append(w_in[:, acc:acc + sz])
        acc += sz
    return out


def _prep_layer(p, l):
    row = lambda a: a[l].reshape(1, -1)
    (w_q, w_k, w_v, w_cq, w_ckv, w_kr, w_u, w_b, w_c, w_cf, w_g) = _split_w_in(p["w_in"][l])
    d = w_q.shape[0]
    half = MLA_ROPE // 2
    pad_hi = HEAD_PAD - MLA_NOPE - MLA_ROPE
    z = lambda n: jnp.zeros((d, n), F32)
    kr_lin = jnp.concatenate([z(MLA_NOPE), w_kr, z(pad_hi)], axis=1)
    kr_swp = jnp.concatenate([z(MLA_NOPE), w_kr[:, half:], w_kr[:, :half], z(pad_hi)], axis=1)
    w_nat = jnp.concatenate([w_u, w_b, w_c, w_cf, w_k, w_cq, w_ckv, kr_lin, kr_swp], axis=1)
    w_t = jnp.concatenate([w_q, w_v], axis=1).T

    w_uq = p["mla_w_uq"][l].reshape(MLA_Q_LORA, MLA_HEADS, MLA_NOPE + MLA_ROPE)
    uq_pad = jnp.pad(w_uq, ((0, 0), (0, 0), (0, pad_hi)))
    rope = w_uq[:, :, MLA_NOPE:]
    uq_swp = jnp.concatenate([rope[:, :, half:], rope[:, :, :half]], axis=2)
    w_ukv = p["mla_w_ukv"][l].reshape(MLA_KV_LORA, MLA_HEADS, MLA_NOPE + MLA_V)
    ukv_k = jnp.pad(w_ukv[:, :, :MLA_NOPE], ((0, 0), (0, 0), (0, HEAD_PAD - MLA_NOPE)))
    ukv_v = w_ukv[:, :, MLA_NOPE:]
    return {
        "g_pre_mix": row(p["g_pre_mix"]),
        "w_nat": w_nat.astype(BF16),
        "w_t": w_t.astype(BF16),
        "w_gates": w_g.astype(BF16),
        "g_cq": row(p["mla_g_cq"]),
        "wuq_t": uq_pad.reshape(MLA_Q_LORA, -1).T.astype(BF16),
        "wuq_swp_t": uq_swp.reshape(MLA_Q_LORA, -1).T.astype(BF16),
        "g_ckv": row(p["mla_g_ckv"]),
        "wukv_k": ukv_k.reshape(MLA_KV_LORA, -1).astype(BF16),
        "wukv_v_t": ukv_v.reshape(MLA_KV_LORA, -1).T.astype(BF16),
        "sc_conv": p["sc_conv"][l],
        "cf_conv": p["cf_conv"][l],
        "cf_conv_b": row(p["cf_conv_b"]),
        "cf_ln_g": row(p["cf_ln_g"]),
        "cf_ln_b": row(p["cf_ln_b"]),
        "lam": [row(p[n]) for n in ("da_lam_q1", "da_lam_k1", "da_lam_q2", "da_lam_k2")],
        "g_subln": p["da_g_subln"][l].reshape(-1, 1),
        "da_w_o": p["da_w_o"][l].astype(BF16),
        "mla_w_o": p["mla_w_o"][l].astype(BF16),
        "sc_w_o": p["sc_w_o"][l].astype(BF16),
        "cf_w_o": p["cf_w_o"][l].astype(BF16),
        "w_mix_out": p["w_mix_out"][l].astype(BF16),
        "g_post_mix": row(p["g_post_mix"]),
        "g_pre_ffn": row(p["g_pre_ffn"]),
        "w_up": p["w_up"][l].astype(BF16),
        "ffn_conv": p["ffn_conv"][l],
        "w_down": p["w_down"][l].astype(BF16),
        "g_post_ffn": row(p["g_post_ffn"]),
    }


def kernel(x, c, positions, w_ada, b_ada, g_pre_mix, w_in, da_lam_q1, da_lam_k1, da_lam_q2, da_lam_k2, da_g_subln, da_w_o, mla_g_cq, mla_w_uq, mla_g_ckv, mla_w_ukv, mla_w_o, sc_conv, sc_w_o, cf_conv, cf_conv_b, cf_ln_g, cf_ln_b, cf_w_o, w_mix_out, g_post_mix, g_pre_ffn, w_up, ffn_conv, w_down, g_post_ffn):
    p = dict(g_pre_mix=g_pre_mix, w_in=w_in, da_lam_q1=da_lam_q1, da_lam_k1=da_lam_k1,
             da_lam_q2=da_lam_q2, da_lam_k2=da_lam_k2, da_g_subln=da_g_subln, da_w_o=da_w_o,
             mla_g_cq=mla_g_cq, mla_w_uq=mla_w_uq, mla_g_ckv=mla_g_ckv, mla_w_ukv=mla_w_ukv,
             mla_w_o=mla_w_o, sc_conv=sc_conv, sc_w_o=sc_w_o, cf_conv=cf_conv,
             cf_conv_b=cf_conv_b, cf_ln_g=cf_ln_g, cf_ln_b=cf_ln_b, cf_w_o=cf_w_o,
             w_mix_out=w_mix_out, g_post_mix=g_post_mix, g_pre_ffn=g_pre_ffn, w_up=w_up,
             ffn_conv=ffn_conv, w_down=w_down, g_post_ffn=g_post_ffn)
    bsz, s, _ = x.shape
    depth = w_ada.shape[0]
    tm = min(TOKEN_TILE, s)
    tq = min(ATTN_TILE, s)
    assert s % tm == 0 and tm % tq == 0

    half = MLA_ROPE // 2
    inv_freq = ROPE_THETA ** (-jnp.arange(half, dtype=F32) / half)
    invf = jnp.broadcast_to(inv_freq[:, None], (half, tm))
    positions3 = positions.reshape(bsz, 1, s)

    mod = _ada_modulation(c, w_ada, b_ada)
    for l in range(depth):
        lw = _prep_layer(p, l)
        lam_init = 0.8 - 0.6 * math.exp(-0.3 * l)
        qt_da, k_da, vt_da, qt_m, k_m, vt_m, yc, yd = _in_proj(
            x, mod[l], positions3, invf, lw, tm=tm, tq=tq)
        ya = _attention(qt_da, k_da, vt_da, lw["lam"] + [lw["g_subln"]],
                        tq=tq, differential=True, lam_init=lam_init)
        yb = _attention(qt_m, k_m, vt_m, [], tq=tq, differential=False)
        x = _merge(x, mod[l], ya, yb, yc, yd, lw, tm=tm)
        x = _ffn(x, mod[l], lw, tm=tm)
    return x
```

```python
import functools
import math

import jax
import jax.numpy as jnp
from jax import lax
from jax.experimental import pallas as pl
from jax.experimental.pallas import tpu as pltpu

D_MODEL = 1024
DA_HEADS = 4
DA_HEAD_DIM = 64
DA_WIDTH = DA_HEADS * 2 * DA_HEAD_DIM
MLA_HEADS = 8
MLA_Q_LORA = 384
MLA_KV_LORA = 256
MLA_NOPE = 64
MLA_ROPE = 32
MLA_V = 64
ROPE_THETA = 10000.0
SC_WIDTH = 512
SC_KERNEL = 3
CF_WIDTH = 512
CF_KERNEL = 31
N_BRANCHES = 4
FFN_DIM = 2816
FFN_KERNEL = 3
N_MOD = 6
NORM_EPS = 1e-6
LN_EPS = 1e-5

HEAD_PAD = 128
LOG2E = math.log2(math.e)
NEG_BIG = -1e30

VMEM_LIMIT_BYTES = 56 * 1024 * 1024
TOKEN_TILE = 512
ATTN_TILE = 256
SUBLANES = 8
CF_ROW_CHUNK = 64
FFN_COL_CHUNK = 256
SC_HALO = 8
CF_HALO = 32
FFN_HALO = 8
ATTN_GROUPS = 4

BF16 = jnp.bfloat16
F32 = jnp.float32


def _dot(a, b):
    return jnp.dot(a, b, preferred_element_type=F32)


def _dot_nt(a, b):
    return lax.dot_general(a, b, (((1,), (1,)), ((), ())), preferred_element_type=F32)


def _sigmoid(x):
    return 1.0 / (1.0 + jnp.exp(-x))


def _rmsnorm(x, g):
    return x * lax.rsqrt(jnp.mean(x * x, axis=-1, keepdims=True) + NORM_EPS) * g


def _const_spec(shape):
    nd = len(shape)
    return pl.BlockSpec(shape, lambda *_: (0,) * nd, pipeline_mode=pl.Buffered(1))


def _layer_spec(arr, l):
    nd = arr.ndim
    return pl.BlockSpec((None,) + arr.shape[1:], lambda *_: (l,) + (0,) * (nd - 1),
                        pipeline_mode=pl.Buffered(1))


def _mod_spec(d, l):
    return pl.BlockSpec((None, 1, N_MOD, d), lambda b, t: (l, b, 0, 0))


def _params(*sem):
    return pltpu.CompilerParams(dimension_semantics=sem, vmem_limit_bytes=VMEM_LIMIT_BYTES)


def _ada_kernel(c_ref, w_ref, b_ref, o_ref):
    c = c_ref[...]
    act = (c * _sigmoid(c)).astype(BF16)
    o_ref[0] = _dot(act, w_ref[0].astype(BF16)) + b_ref[0]


def _ada_modulation(c, w_ada, b_ada):
    depth, d, n = w_ada.shape
    bsz = c.shape[0]
    out = pl.pallas_call(
        _ada_kernel,
        out_shape=jax.ShapeDtypeStruct((depth, bsz, n), F32),
        grid=(depth, n // d),
        in_specs=[
            pl.BlockSpec((bsz, d), lambda l, j: (0, 0)),
            pl.BlockSpec((1, d, d), lambda l, j: (l, 0, j)),
            pl.BlockSpec((1, 1, d), lambda l, j: (l, 0, j)),
        ],
        out_specs=pl.BlockSpec((1, bsz, d), lambda l, j: (l, 0, j)),
        compiler_params=_params("arbitrary", "arbitrary"),
        name="ada_modulation",
    )(c, w_ada, b_ada.reshape(depth, 1, n))
    return out.reshape(depth, bsz, N_MOD, d)


_C_SC = 0
_C_CF = _C_SC + 3 * SC_WIDTH
_C_DAK = _C_CF + 2 * CF_WIDTH
_C_CQ = _C_DAK + DA_WIDTH
_C_END = _C_CQ + MLA_Q_LORA + MLA_KV_LORA + 2 * HEAD_PAD


def _in_proj_kernel(x_ref, mod_ref, gpre_ref, pos_ref, invf_ref, wnat_ref, wt_ref,
                    gcq_ref, wuqt_ref, wuqswt_ref, gckv_ref, wukvk_ref, wukvvt_ref,
                    scconv_ref, cfconv_ref, cfb_ref, lng_ref, lnb_ref,
                    qtda_ref, kda_ref, vtda_ref, qtm_ref, km_ref, vtm_ref, yc_ref, yd_ref,
                    wbuf, ubuf, ushift, *, tm, tq):
    t = pl.program_id(1)

    @pl.when(t == 0)
    def _():
        wbuf[0:SC_HALO, :] = jnp.zeros((SC_HALO, SC_WIDTH), F32)
        ubuf[0:CF_HALO, :] = jnp.zeros((CF_HALO, CF_WIDTH), F32)

    x = x_ref[0]
    shift = mod_ref[0, 0:1, :]
    scale = mod_ref[0, 1:2, :]
    h = (_rmsnorm(x, gpre_ref[...]) * (1.0 + scale) + shift).astype(BF16)

    sc = _dot(h, wnat_ref[:, _C_SC:_C_CF])
    w = sc[:, 2 * SC_WIDTH:3 * SC_WIDTH] * sc[:, 0:SC_WIDTH]
    wbuf[SC_HALO:SC_HALO + tm, :] = w
    conv = (scconv_ref[2:3, :] * w
            + scconv_ref[1:2, :] * wbuf[SC_HALO - 1:SC_HALO - 1 + tm, :]
            + scconv_ref[0:1, :] * wbuf[SC_HALO - 2:SC_HALO - 2 + tm, :])
    yc_ref[0] = (sc[:, SC_WIDTH:2 * SC_WIDTH] * conv).astype(BF16)
    wbuf[0:SC_HALO, :] = wbuf[tm:tm + SC_HALO, :]

    cf = _dot(h, wnat_ref[:, _C_CF:_C_DAK])
    ubuf[CF_HALO:CF_HALO + tm, :] = cf[:, 0:CF_WIDTH] * _sigmoid(cf[:, CF_WIDTH:2 * CF_WIDTH])
    for b in range(1, SUBLANES):
        ushift[b - 1] = pltpu.roll(ubuf[...], b, axis=0)
    for r0 in range(0, tm, CF_ROW_CHUNK):
        acc = jnp.broadcast_to(cfb_ref[...], (CF_ROW_CHUNK, CF_WIDTH))
        for s in range(CF_KERNEL):
            a, b = divmod(s, SUBLANES)
            lo = CF_HALO + r0 - SUBLANES * a
            src = ubuf[lo:lo + CF_ROW_CHUNK, :] if b == 0 else ushift[b - 1, lo:lo + CF_ROW_CHUNK, :]
            acc = acc + cfconv_ref[CF_KERNEL - 1 - s:CF_KERNEL - s, :] * src
        mu = jnp.mean(acc, axis=-1, keepdims=True)
        cen = acc - mu
        var = jnp.mean(cen * cen, axis=-1, keepdims=True)
        y = cen * lax.rsqrt(var + LN_EPS) * lng_ref[...] + lnb_ref[...]
        yd_ref[0, r0:r0 + CF_ROW_CHUNK, :] = (y * _sigmoid(y)).astype(BF16)
    ubuf[0:CF_HALO, :] = ubuf[tm:tm + CF_HALO, :]

    kda_ref[0] = _dot(h, wnat_ref[:, _C_DAK:_C_CQ]).astype(BF16)
    tr = _dot_nt(wt_ref[...], h)
    qs_da = DA_HEAD_DIM ** -0.5 * LOG2E
    for c in range(tm // tq):
        qtda_ref[0, c] = (tr[0:DA_WIDTH, c * tq:(c + 1) * tq] * qs_da).astype(BF16)
        vtda_ref[0, c] = tr[DA_WIDTH:2 * DA_WIDTH, c * tq:(c + 1) * tq].astype(BF16)

    lat = _dot(h, wnat_ref[:, _C_CQ:_C_END])
    o_kv = MLA_Q_LORA
    o_kr = o_kv + MLA_KV_LORA
    cqn = _rmsnorm(lat[:, 0:o_kv], gcq_ref[...]).astype(BF16)
    ckvn = _rmsnorm(lat[:, o_kv:o_kr], gckv_ref[...]).astype(BF16)
    kr_lin = lat[:, o_kr:o_kr + HEAD_PAD]
    kr_swp = lat[:, o_kr + HEAD_PAD:o_kr + 2 * HEAD_PAD]

    half = MLA_ROPE // 2
    ang = invf_ref[...] * pos_ref[0].astype(F32)
    cos_h = jnp.cos(ang)
    sin_h = jnp.sin(ang)
    cos_r = jnp.concatenate([cos_h, cos_h], axis=0)
    sin_r = jnp.concatenate([-sin_h, sin_h], axis=0)

    zeros_lo = jnp.zeros((MLA_NOPE, tm), F32)
    zeros_hi = jnp.zeros((HEAD_PAD - MLA_NOPE - MLA_ROPE, tm), F32)
    cos_n = jnp.concatenate([zeros_lo, cos_r, zeros_hi], axis=0).T
    sin_n = jnp.concatenate([zeros_lo, sin_r, zeros_hi], axis=0).T
    k_rope = kr_lin * cos_n + kr_swp * sin_n
    k_nope = _dot(ckvn, wukvk_ref[...])
    for hd in range(MLA_HEADS):
        km_ref[0, :, hd * HEAD_PAD:(hd + 1) * HEAD_PAD] = (
            k_nope[:, hd * HEAD_PAD:(hd + 1) * HEAD_PAD] + k_rope).astype(BF16)

    vt = _dot_nt(wukvvt_ref[...], ckvn)
    qt_lin = _dot_nt(wuqt_ref[...], cqn)
    qt_swp = _dot_nt(wuqswt_ref[...], cqn)
    qs_mla = (MLA_NOPE + MLA_ROPE) ** -0.5 * LOG2E
    pad_rows = jnp.zeros((HEAD_PAD - MLA_NOPE - MLA_ROPE, tm), F32)
    qt_heads = []
    for hd in range(MLA_HEADS):
        lin = qt_lin[hd * HEAD_PAD:(hd + 1) * HEAD_PAD]
        rot = (lin[MLA_NOPE:MLA_NOPE + MLA_ROPE] * cos_r
               + qt_swp[hd * MLA_ROPE:(hd + 1) * MLA_ROPE] * sin_r)
        qt_heads.append(jnp.concatenate([lin[0:MLA_NOPE], rot, pad_rows], axis=0))
    qt = jnp.concatenate(qt_heads, axis=0) * qs_mla
    for c in range(tm // tq):
        qtm_ref[0, c] = qt[:, c * tq:(c + 1) * tq].astype(BF16)
        vtm_ref[0, c] = vt[:, c * tq:(c + 1) * tq].astype(BF16)


def _in_proj(x, mod, positions3, invf, w, l, *, tm, tq):
    bsz, s, d = x.shape
    nt = s // tm
    nq = s // tq
    kern = functools.partial(_in_proj_kernel, tm=tm, tq=tq)
    names = ["g_pre_mix", None, None, "w_nat", "w_t", "g_cq", "wuq_t", "wuq_swp_t", "g_ckv",
             "wukv_k", "wukv_v_t", "sc_conv", "cf_conv", "cf_conv_b", "cf_ln_g", "cf_ln_b"]
    in_specs = [pl.BlockSpec((1, tm, d), lambda b, t: (b, t, 0)), _mod_spec(d, l)]
    args = [x, mod]
    fixed = iter([(positions3, pl.BlockSpec((1, 1, tm), lambda b, t: (b, 0, t))),
                  (invf, _const_spec(invf.shape))])
    for n in names:
        arr, spec = next(fixed) if n is None else (w[n], _layer_spec(w[n], l))
        in_specs.append(spec)
        args.append(arr)
    slab = tm // tq
    out_shape = (
        jax.ShapeDtypeStruct((bsz, nq, DA_WIDTH, tq), BF16),
        jax.ShapeDtypeStruct((bsz, s, DA_WIDTH), BF16),
        jax.ShapeDtypeStruct((bsz, nq, DA_WIDTH, tq), BF16),
        jax.ShapeDtypeStruct((bsz, nq, MLA_HEADS * HEAD_PAD, tq), BF16),
        jax.ShapeDtypeStruct((bsz, s, MLA_HEADS * HEAD_PAD), BF16),
        jax.ShapeDtypeStruct((bsz, nq, MLA_HEADS * MLA_V, tq), BF16),
        jax.ShapeDtypeStruct((bsz, s, SC_WIDTH), BF16),
        jax.ShapeDtypeStruct((bsz, s, CF_WIDTH), BF16),
    )

    def tspec(rows):
        return pl.BlockSpec((1, slab, rows, tq), lambda b, t: (b, t, 0, 0))

    def nspec(cols):
        return pl.BlockSpec((1, tm, cols), lambda b, t: (b, t, 0))

    out_specs = (tspec(DA_WIDTH), nspec(DA_WIDTH), tspec(DA_WIDTH),
                 tspec(MLA_HEADS * HEAD_PAD), nspec(MLA_HEADS * HEAD_PAD),
                 tspec(MLA_HEADS * MLA_V), nspec(SC_WIDTH), nspec(CF_WIDTH))
    return pl.pallas_call(
        kern,
        out_shape=out_shape,
        grid=(bsz, nt),
        in_specs=in_specs,
        out_specs=out_specs,
        scratch_shapes=[pltpu.VMEM((tm + SC_HALO, SC_WIDTH), F32),
                        pltpu.VMEM((tm + CF_HALO, CF_WIDTH), F32),
                        pltpu.VMEM((SUBLANES - 1, tm + CF_HALO, CF_WIDTH), F32)],
        compiler_params=_params("arbitrary", "arbitrary"),
        name="in_proj",
    )(*args)


def _attn_kernel(*refs, tq, differential, lam_init):
    if differential:
        (qt_ref, qtn_ref, k_ref, vt_ref, lq1_ref, lk1_ref, lq2_ref, lk2_ref, gsub_ref,
         o_ref, qcat_ref, m_ref, l_ref, acc_ref, s_ref, smax_ref) = refs
    else:
        (qt_ref, qtn_ref, k_ref, vt_ref,
         o_ref, qcat_ref, m_ref, l_ref, acc_ref, s_ref, smax_ref) = refs
    i = pl.program_id(1)
    tk = tq
    qrows = qt_ref.shape[2] // ATTN_GROUPS

    for g in range(ATTN_GROUPS):
        m_ref[g] = jnp.full(m_ref.shape[1:], NEG_BIG, F32)
        l_ref[g] = jnp.zeros(l_ref.shape[1:], F32)
        acc_ref[g] = jnp.zeros(acc_ref.shape[1:], F32)

    def load_queries(src_ref):
        if not differential:
            return
        for g in range(ATTN_GROUPS):
            qt = src_ref[0, 0, g * qrows:(g + 1) * qrows, :]
            row = lax.broadcasted_iota(jnp.int32, qt.shape, 0)
            zero = jnp.zeros_like(qt)
            qcat_ref[g, :, 0:tq] = jnp.where(row < DA_HEAD_DIM, qt, zero)
            qcat_ref[g, :, tq:2 * tq] = jnp.where(row >= DA_HEAD_DIM, qt, zero)

    def scores(j, slot, src_ref=qt_ref):
        k0 = pl.multiple_of(j * tk, tk)
        for g in range(ATTN_GROUPS):
            c0 = g * qrows
            if differential:
                s = _dot(k_ref[0, pl.ds(k0, tk), c0:c0 + qrows], qcat_ref[g])
                s_ref[slot, g] = s
                smax_ref[slot, g] = jnp.max(s, axis=0, keepdims=True)
            else:
                for n in range(2):
                    r0 = c0 + n * HEAD_PAD
                    s = _dot(k_ref[0, pl.ds(k0, tk), r0:r0 + HEAD_PAD],
                             src_ref[0, 0, r0:r0 + HEAD_PAD, :])
                    s_ref[slot, g, :, n * tq:(n + 1) * tq] = s
                    smax_ref[slot, g, :, n * tq:(n + 1) * tq] = jnp.max(s, axis=0, keepdims=True)

    def softmax_pv(j, slot, masked):
        for g in range(ATTN_GROUPS):
            s = s_ref[slot, g]
            if masked:
                kpos = lax.broadcasted_iota(jnp.int32, (tk, tq), 0)
                qpos = lax.broadcasted_iota(jnp.int32, (tk, tq), 1)
                keep = kpos <= qpos
                keep = jnp.concatenate([keep, keep], axis=1)
                s = jnp.where(keep, s, NEG_BIG)
                smax = jnp.max(s, axis=0, keepdims=True)
            else:
                smax = smax_ref[slot, g]
            m_old = m_ref[g]
            m_new = jnp.maximum(m_old, smax)
            alpha = jnp.exp2(m_old - m_new)
            p = jnp.exp2(s - m_new)
            l_ref[g] = alpha * l_ref[g] + jnp.sum(p, axis=0, keepdims=True)
            m_ref[g] = m_new
            vt = vt_ref[0, j, g * HEAD_PAD:(g + 1) * HEAD_PAD, :]
            acc_ref[g] = alpha * acc_ref[g] + _dot(vt, p.astype(BF16))

    @pl.when(i == 0)
    def _():
        load_queries(qt_ref)
        scores(0, 0)

    def body(jj, carry):
        j = 2 * jj
        scores(j + 1, 1)
        softmax_pv(j, 0, False)
        scores(j + 2, 0)
        softmax_pv(j + 1, 1, False)
        return carry

    lax.fori_loop(0, i // 2, body, 0)

    @pl.when(i % 2 == 1)
    def _():
        scores(i, 1)
        softmax_pv(i - 1, 0, False)

    softmax_pv(i, i % 2, True)

    load_queries(qtn_ref)
    scores(0, 0, qtn_ref)

    if differential:
        lam = (jnp.exp(jnp.sum(lq1_ref[...] * lk1_ref[...], axis=-1, keepdims=True))
               - jnp.exp(jnp.sum(lq2_ref[...] * lk2_ref[...], axis=-1, keepdims=True)) + lam_init)
    for g in range(ATTN_GROUPS):
        acc = acc_ref[g]
        l = l_ref[g]
        if differential:
            o = acc[:, 0:tq] / l[:, 0:tq] - lam * (acc[:, tq:2 * tq] / l[:, tq:2 * tq])
            o = o * lax.rsqrt(jnp.mean(o * o, axis=0, keepdims=True) + NORM_EPS)
            o = o * gsub_ref[...] * (1.0 - lam_init)
        else:
            o = jnp.concatenate([acc[0:MLA_V, 0:tq] / l[:, 0:tq],
                                 acc[MLA_V:2 * MLA_V, tq:2 * tq] / l[:, tq:2 * tq]], axis=0)
        o_ref[0, :, g * HEAD_PAD:(g + 1) * HEAD_PAD] = o.T.astype(BF16)


def _attention(qt, k, vt, extra, extra_specs, *, tq, differential, lam_init=0.0):
    bsz, nq, rows, _ = qt.shape
    s = k.shape[1]
    width = ATTN_GROUPS * HEAD_PAD
    kern = functools.partial(_attn_kernel, tq=tq, differential=differential, lam_init=lam_init)
    in_specs = [
        pl.BlockSpec((1, 1, rows, tq), lambda b, i: (b, i, 0, 0)),
        pl.BlockSpec((1, 1, rows, tq), lambda b, i: (b, jnp.minimum(i + 1, nq - 1), 0, 0)),
        pl.BlockSpec((1, s, rows), lambda b, i: (b, 0, 0)),
        pl.BlockSpec((1, nq, width, tq), lambda b, i: (b, 0, 0, 0)),
    ]
    in_specs += extra_specs
    return pl.pallas_call(
        kern,
        out_shape=jax.ShapeDtypeStruct((bsz, s, width), BF16),
        grid=(bsz, nq),
        in_specs=in_specs,
        out_specs=pl.BlockSpec((1, tq, width), lambda b, i: (b, i, 0)),
        scratch_shapes=[pltpu.VMEM((ATTN_GROUPS, HEAD_PAD, 2 * tq), BF16),
                        pltpu.VMEM((ATTN_GROUPS, 1, 2 * tq), F32),
                        pltpu.VMEM((ATTN_GROUPS, 1, 2 * tq), F32),
                        pltpu.VMEM((ATTN_GROUPS, HEAD_PAD, 2 * tq), F32),
                        pltpu.VMEM((2, ATTN_GROUPS, tq, 2 * tq), F32),
                        pltpu.VMEM((2, ATTN_GROUPS, 1, 2 * tq), F32)],
        compiler_params=_params("arbitrary", "arbitrary"),
        name="diff_attention" if differential else "latent_attention",
    )(qt, qt, k, vt, *extra)


def _merge_kernel(x_ref, mod_ref, gpre_ref, wg_ref, ya_ref, yb_ref, yc_ref, yd_ref,
                  woa_ref, wob_ref, woc_ref, wod_ref, wmix_ref, gpost_ref, o_ref):
    x = x_ref[0]
    shift = mod_ref[0, 0:1, :]
    scale = mod_ref[0, 1:2, :]
    gate = mod_ref[0, 2:3, :]
    h = (_rmsnorm(x, gpre_ref[...]) * (1.0 + scale) + shift).astype(BF16)
    merged = None
    branches = ((ya_ref, woa_ref), (yb_ref, wob_ref), (yc_ref, woc_ref), (yd_ref, wod_ref))
    for n, (pre_ref, wo_ref) in enumerate(branches):
        g = _sigmoid(_dot(h, wg_ref[:, n * D_MODEL:(n + 1) * D_MODEL]))
        term = g * _dot(pre_ref[0], wo_ref[...])
        merged = term if merged is None else merged + term
    z = _dot(merged.astype(BF16), wmix_ref[...])
    o_ref[0] = x + gate * _rmsnorm(z, gpost_ref[...])


def _merge(x, mod, ya, yb, yc, yd, w, l, *, tm):
    bsz, s, d = x.shape

    def tok(cols):
        return pl.BlockSpec((1, tm, cols), lambda b, t: (b, t, 0))

    consts_a = [w["g_pre_mix"], w["w_gates"]]
    consts_b = [w["da_w_o"], w["mla_w_o"], w["sc_w_o"], w["cf_w_o"], w["w_mix_out"],
                w["g_post_mix"]]
    in_specs = ([tok(d), _mod_spec(d, l)]
                + [_layer_spec(a, l) for a in consts_a]
                + [tok(a.shape[-1]) for a in (ya, yb, yc, yd)]
                + [_layer_spec(a, l) for a in consts_b])
    return pl.pallas_call(
        _merge_kernel,
        out_shape=jax.ShapeDtypeStruct(x.shape, F32),
        grid=(bsz, s // tm),
        in_specs=in_specs,
        out_specs=tok(d),
        compiler_params=_params("arbitrary", "arbitrary"),
        name="merge_out_proj",
    )(x, mod, *consts_a, ya, yb, yc, yd, *consts_b)


def _ffn_kernel(x_ref, mod_ref, gpre_ref, wup_ref, conv_ref, wdown_ref, gpost_ref, o_ref,
                abuf, carry, act_ref, *, tm):
    t = pl.program_id(1)

    @pl.when(t == 0)
    def _():
        carry[...] = jnp.zeros(carry.shape, F32)

    x = x_ref[0]
    shift = mod_ref[0, 3:4, :]
    scale = mod_ref[0, 4:5, :]
    gate = mod_ref[0, 5:6, :]
    h = (_rmsnorm(x, gpre_ref[...]) * (1.0 + scale) + shift).astype(BF16)
    fc = FFN_COL_CHUNK
    for c0 in range(0, FFN_DIM, fc):
        a = _dot(h, wup_ref[:, c0:c0 + fc])
        b = _dot(h, wup_ref[:, FFN_DIM + c0:FFN_DIM + c0 + fc])
        abuf[0:FFN_HALO, :] = carry[:, c0:c0 + fc]
        abuf[FFN_HALO:FFN_HALO + tm, :] = a
        carry[:, c0:c0 + fc] = a[tm - FFN_HALO:tm, :]
        cv = (conv_ref[2:3, c0:c0 + fc] * a
              + conv_ref[1:2, c0:c0 + fc] * abuf[FFN_HALO - 1:FFN_HALO - 1 + tm, :]
              + conv_ref[0:1, c0:c0 + fc] * abuf[FFN_HALO - 2:FFN_HALO - 2 + tm, :])
        act_ref[:, c0:c0 + fc] = (cv * _sigmoid(cv) * b).astype(BF16)
    y = _dot(act_ref[...], wdown_ref[...])
    o_ref[0] = x + gate * _rmsnorm(y, gpost_ref[...])


def _ffn(x, mod, w, l, *, tm):
    bsz, s, d = x.shape
    tok = pl.BlockSpec((1, tm, d), lambda b, t: (b, t, 0))
    consts = [w["g_pre_ffn"], w["w_up"], w["ffn_conv"], w["w_down"], w["g_post_ffn"]]
    return pl.pallas_call(
        functools.partial(_ffn_kernel, tm=tm),
        out_shape=jax.ShapeDtypeStruct(x.shape, F32),
        grid=(bsz, s // tm),
        in_specs=[tok, _mod_spec(d, l)] + [_layer_spec(a, l) for a in consts],
        out_specs=tok,
        scratch_shapes=[pltpu.VMEM((tm + FFN_HALO, FFN_COL_CHUNK), F32),
                        pltpu.VMEM((FFN_HALO, FFN_DIM), F32),
                        pltpu.VMEM((tm, FFN_DIM), BF16)],
        compiler_params=_params("arbitrary", "arbitrary"),
        name="gated_mlp",
    )(x, mod, *consts)


def _split_w_in(w_in):
    sizes = (DA_WIDTH, DA_WIDTH, DA_WIDTH, MLA_Q_LORA, MLA_KV_LORA, MLA_ROPE,
             SC_WIDTH, SC_WIDTH, SC_WIDTH, 2 * CF_WIDTH, N_BRANCHES * D_MODEL)
    out, acc = [], 0
    for sz in sizes:
        out.append(w_in[..., acc:acc + sz])
        acc += sz
    return out


def _prep_weights(p):
    depth = p["w_in"].shape[0]
    row = lambda a: a.reshape(depth, 1, -1)
    tr = lambda a: jnp.swapaxes(a, 1, 2)
    (w_q, w_k, w_v, w_cq, w_ckv, w_kr, w_u, w_b, w_c, w_cf, w_g) = _split_w_in(p["w_in"])
    d = w_q.shape[1]
    half = MLA_ROPE // 2
    pad_hi = HEAD_PAD - MLA_NOPE - MLA_ROPE
    z = lambda n: jnp.zeros((depth, d, n), F32)
    kr_lin = jnp.concatenate([z(MLA_NOPE), w_kr, z(pad_hi)], axis=2)
    kr_swp = jnp.concatenate([z(MLA_NOPE), w_kr[..., half:], w_kr[..., :half], z(pad_hi)], axis=2)
    w_nat = jnp.concatenate([w_u, w_b, w_c, w_cf, w_k, w_cq, w_ckv, kr_lin, kr_swp], axis=2)
    w_t = tr(jnp.concatenate([w_q, w_v], axis=2))

    w_uq = p["mla_w_uq"].reshape(depth, MLA_Q_LORA, MLA_HEADS, MLA_NOPE + MLA_ROPE)
    uq_pad = jnp.pad(w_uq, ((0, 0), (0, 0), (0, 0), (0, pad_hi)))
    rope = w_uq[..., MLA_NOPE:]
    uq_swp = jnp.concatenate([rope[..., half:], rope[..., :half]], axis=3)
    w_ukv = p["mla_w_ukv"].reshape(depth, MLA_KV_LORA, MLA_HEADS, MLA_NOPE + MLA_V)
    ukv_k = jnp.pad(w_ukv[..., :MLA_NOPE], ((0, 0), (0, 0), (0, 0), (0, HEAD_PAD - MLA_NOPE)))
    ukv_v = w_ukv[..., MLA_NOPE:]
    return {
        "g_pre_mix": row(p["g_pre_mix"]),
        "w_nat": w_nat.astype(BF16),
        "w_t": w_t.astype(BF16),
        "w_gates": w_g.astype(BF16),
        "g_cq": row(p["mla_g_cq"]),
        "wuq_t": tr(uq_pad.reshape(depth, MLA_Q_LORA, -1)).astype(BF16),
        "wuq_swp_t": tr(uq_swp.reshape(depth, MLA_Q_LORA, -1)).astype(BF16),
        "g_ckv": row(p["mla_g_ckv"]),
        "wukv_k": ukv_k.reshape(depth, MLA_KV_LORA, -1).astype(BF16),
        "wukv_v_t": tr(ukv_v.reshape(depth, MLA_KV_LORA, -1)).astype(BF16),
        "sc_conv": p["sc_conv"],
        "cf_conv": p["cf_conv"],
        "cf_conv_b": row(p["cf_conv_b"]),
        "cf_ln_g": row(p["cf_ln_g"]),
        "cf_ln_b": row(p["cf_ln_b"]),
        "lam": [row(p[n]) for n in ("da_lam_q1", "da_lam_k1", "da_lam_q2", "da_lam_k2")],
        "g_subln": p["da_g_subln"].reshape(depth, -1, 1),
        "da_w_o": p["da_w_o"].astype(BF16),
        "mla_w_o": p["mla_w_o"].astype(BF16),
        "sc_w_o": p["sc_w_o"].astype(BF16),
        "cf_w_o": p["cf_w_o"].astype(BF16),
        "w_mix_out": p["w_mix_out"].astype(BF16),
        "g_post_mix": row(p["g_post_mix"]),
        "g_pre_ffn": row(p["g_pre_ffn"]),
        "w_up": p["w_up"].astype(BF16),
        "ffn_conv": p["ffn_conv"],
        "w_down": p["w_down"].astype(BF16),
        "g_post_ffn": row(p["g_post_ffn"]),
    }


def kernel(x, c, positions, w_ada, b_ada, g_pre_mix, w_in, da_lam_q1, da_lam_k1, da_lam_q2, da_lam_k2, da_g_subln, da_w_o, mla_g_cq, mla_w_uq, mla_g_ckv, mla_w_ukv, mla_w_o, sc_conv, sc_w_o, cf_conv, cf_conv_b, cf_ln_g, cf_ln_b, cf_w_o, w_mix_out, g_post_mix, g_pre_ffn, w_up, ffn_conv, w_down, g_post_ffn):
    p = dict(g_pre_mix=g_pre_mix, w_in=w_in, da_lam_q1=da_lam_q1, da_lam_k1=da_lam_k1,
             da_lam_q2=da_lam_q2, da_lam_k2=da_lam_k2, da_g_subln=da_g_subln, da_w_o=da_w_o,
             mla_g_cq=mla_g_cq, mla_w_uq=mla_w_uq, mla_g_ckv=mla_g_ckv, mla_w_ukv=mla_w_ukv,
             mla_w_o=mla_w_o, sc_conv=sc_conv, sc_w_o=sc_w_o, cf_conv=cf_conv,
             cf_conv_b=cf_conv_b, cf_ln_g=cf_ln_g, cf_ln_b=cf_ln_b, cf_w_o=cf_w_o,
             w_mix_out=w_mix_out, g_post_mix=g_post_mix, g_pre_ffn=g_pre_ffn, w_up=w_up,
             ffn_conv=ffn_conv, w_down=w_down, g_post_ffn=g_post_ffn)
    bsz, s, _ = x.shape
    depth = w_ada.shape[0]
    tm = min(TOKEN_TILE, s)
    tq = min(ATTN_TILE, s)
    assert s % tm == 0 and tm % tq == 0

    half = MLA_ROPE // 2
    inv_freq = ROPE_THETA ** (-jnp.arange(half, dtype=F32) / half)
    invf = jnp.broadcast_to(inv_freq[:, None], (half, tm))
    positions3 = positions.reshape(bsz, 1, s)

    mod = _ada_modulation(c, w_ada, b_ada)
    w = _prep_weights(p)
    for l in range(depth):
        lam_init = 0.8 - 0.6 * math.exp(-0.3 * l)
        qt_da, k_da, vt_da, qt_m, k_m, vt_m, yc, yd = _in_proj(
            x, mod, positions3, invf, w, l, tm=tm, tq=tq)
        da_extra = w["lam"] + [w["g_subln"]]
        ya = _attention(qt_da, k_da, vt_da, da_extra, [_layer_spec(a, l) for a in da_extra],
                        tq=tq, differential=True, lam_init=lam_init)
        yb = _attention(qt_m, k_m, vt_m, [], [], tq=tq, differential=False)
        x = _merge(x, mod, ya, yb, yc, yd, w, l, tm=tm)
        x = _ffn(x, mod, w, l, tm=tm)
    return x
```

```python
import functools
import math

import jax
import jax.numpy as jnp
from jax import lax
from jax.experimental import pallas as pl
from jax.experimental.pallas import tpu as pltpu

D_MODEL = 1024
DA_HEADS = 4
DA_HEAD_DIM = 64
DA_WIDTH = DA_HEADS * 2 * DA_HEAD_DIM
MLA_HEADS = 8
MLA_Q_LORA = 384
MLA_KV_LORA = 256
MLA_NOPE = 64
MLA_ROPE = 32
MLA_V = 64
ROPE_THETA = 10000.0
SC_WIDTH = 512
SC_KERNEL = 3
CF_WIDTH = 512
CF_KERNEL = 31
N_BRANCHES = 4
FFN_DIM = 2816
FFN_KERNEL = 3
N_MOD = 6
NORM_EPS = 1e-6
LN_EPS = 1e-5

HEAD_PAD = 128
LOG2E = math.log2(math.e)
NEG_BIG = -1e30

VMEM_LIMIT_BYTES = 56 * 1024 * 1024
TOKEN_TILE = 512
MIX_TOKEN_TILE = 1024
ATTN_TILE = 256
SUBLANES = 8
CF_ROW_CHUNK = 64
FFN_COL_CHUNK = 256
SC_HALO = 8
CF_HALO = 32
FFN_HALO = 8
ONES_ROWS = 16
ATTN_GROUPS = 4

BF16 = jnp.bfloat16
F32 = jnp.float32


def _dot(a, b):
    return jnp.dot(a, b, preferred_element_type=F32)


def _dot_nt(a, b):
    return lax.dot_general(a, b, (((1,), (1,)), ((), ())), preferred_element_type=F32)


def _sigmoid(x):
    return 1.0 / (1.0 + jnp.exp(-x))


def _rmsnorm(x, g):
    return x * lax.rsqrt(jnp.mean(x * x, axis=-1, keepdims=True) + NORM_EPS) * g


def _const_spec(shape):
    nd = len(shape)
    return pl.BlockSpec(shape, lambda *_: (0,) * nd, pipeline_mode=pl.Buffered(1))


def _layer_spec(arr, l):
    nd = arr.ndim
    return pl.BlockSpec((None,) + arr.shape[1:], lambda *_: (l,) + (0,) * (nd - 1),
                        pipeline_mode=pl.Buffered(1))


def _mod_spec(d, l):
    return pl.BlockSpec((None, 1, N_MOD, d), lambda b, t: (l, b, 0, 0))


def _params(*sem):
    return pltpu.CompilerParams(dimension_semantics=sem, vmem_limit_bytes=VMEM_LIMIT_BYTES)


def _ada_kernel(c_ref, w_ref, b_ref, o_ref):
    c = c_ref[...]
    act = (c * _sigmoid(c)).astype(BF16)
    o_ref[0] = _dot(act, w_ref[0].astype(BF16)) + b_ref[0]


def _ada_modulation(c, w_ada, b_ada):
    depth, d, n = w_ada.shape
    bsz = c.shape[0]
    out = pl.pallas_call(
        _ada_kernel,
        out_shape=jax.ShapeDtypeStruct((depth, bsz, n), F32),
        grid=(depth, n // d),
        in_specs=[
            pl.BlockSpec((bsz, d), lambda l, j: (0, 0)),
            pl.BlockSpec((1, d, d), lambda l, j: (l, 0, j)),
            pl.BlockSpec((1, 1, d), lambda l, j: (l, 0, j)),
        ],
        out_specs=pl.BlockSpec((1, bsz, d), lambda l, j: (l, 0, j)),
        compiler_params=_params("arbitrary", "arbitrary"),
        name="ada_modulation",
    )(c, w_ada, b_ada.reshape(depth, 1, n))
    return out.reshape(depth, bsz, N_MOD, d)


_C_SC = 0
_C_CF = _C_SC + 3 * SC_WIDTH
_C_DAK = _C_CF + 2 * CF_WIDTH
_C_CQ = _C_DAK + DA_WIDTH
_C_END = _C_CQ + MLA_Q_LORA + MLA_KV_LORA + 2 * HEAD_PAD


def _in_proj_kernel(x_ref, mod_ref, gpre_ref, pos_ref, invf_ref, wnat_ref, wt_ref,
                    gcq_ref, wuqt_ref, wuqswt_ref, gckv_ref, wukvk_ref, wukvvt_ref,
                    scconv_ref, cfconv_ref, cfb_ref, lng_ref, lnb_ref,
                    qtda_ref, kda_ref, vtda_ref, qtm_ref, km_ref, vtm_ref, yc_ref, yd_ref,
                    wbuf, ubuf, ushift, *, tm, tq):
    t = pl.program_id(1)

    @pl.when(t == 0)
    def _():
        wbuf[0:SC_HALO, :] = jnp.zeros((SC_HALO, SC_WIDTH), F32)
        ubuf[0:CF_HALO, :] = jnp.zeros((CF_HALO, CF_WIDTH), F32)

    x = x_ref[0]
    shift = mod_ref[0, 0:1, :]
    scale = mod_ref[0, 1:2, :]
    h = (_rmsnorm(x, gpre_ref[...]) * (1.0 + scale) + shift).astype(BF16)

    cf = _dot(h, wnat_ref[:, _C_CF:_C_DAK])
    ubuf[CF_HALO:CF_HALO + tm, :] = cf[:, 0:CF_WIDTH] * _sigmoid(cf[:, CF_WIDTH:2 * CF_WIDTH])
    for b in range(1, SUBLANES):
        ushift[b - 1] = pltpu.roll(ubuf[...], b, axis=0)
    for r0 in range(0, tm, CF_ROW_CHUNK):
        acc = jnp.broadcast_to(cfb_ref[...], (CF_ROW_CHUNK, CF_WIDTH))
        for s in range(CF_KERNEL):
            a, b = divmod(s, SUBLANES)
            lo = CF_HALO + r0 - SUBLANES * a
            src = ubuf[lo:lo + CF_ROW_CHUNK, :] if b == 0 else ushift[b - 1, lo:lo + CF_ROW_CHUNK, :]
            acc = acc + cfconv_ref[CF_KERNEL - 1 - s:CF_KERNEL - s, :] * src
        mu = jnp.mean(acc, axis=-1, keepdims=True)
        cen = acc - mu
        var = jnp.mean(cen * cen, axis=-1, keepdims=True)
        y = cen * lax.rsqrt(var + LN_EPS) * lng_ref[...] + lnb_ref[...]
        yd_ref[0, r0:r0 + CF_ROW_CHUNK, :] = (y * _sigmoid(y)).astype(BF16)
    ubuf[0:CF_HALO, :] = ubuf[tm:tm + CF_HALO, :]

    sc = _dot(h, wnat_ref[:, _C_SC:_C_CF])
    w = sc[:, 2 * SC_WIDTH:3 * SC_WIDTH] * sc[:, 0:SC_WIDTH]
    wbuf[SC_HALO:SC_HALO + tm, :] = w
    conv = (scconv_ref[2:3, :] * w
            + scconv_ref[1:2, :] * wbuf[SC_HALO - 1:SC_HALO - 1 + tm, :]
            + scconv_ref[0:1, :] * wbuf[SC_HALO - 2:SC_HALO - 2 + tm, :])
    yc_ref[0] = (sc[:, SC_WIDTH:2 * SC_WIDTH] * conv).astype(BF16)
    wbuf[0:SC_HALO, :] = wbuf[tm:tm + SC_HALO, :]

    kda_ref[0] = _dot(h, wnat_ref[:, _C_DAK:_C_CQ]).astype(BF16)
    tr = _dot_nt(wt_ref[...], h)
    qs_da = DA_HEAD_DIM ** -0.5 * LOG2E
    for c in range(tm // tq):
        qtda_ref[0, c] = (tr[0:DA_WIDTH, c * tq:(c + 1) * tq] * qs_da).astype(BF16)
        vtda_ref[0, c] = tr[DA_WIDTH:2 * DA_WIDTH, c * tq:(c + 1) * tq].astype(BF16)

    lat = _dot(h, wnat_ref[:, _C_CQ:_C_END])
    o_kv = MLA_Q_LORA
    o_kr = o_kv + MLA_KV_LORA
    cqn = _rmsnorm(lat[:, 0:o_kv], gcq_ref[...]).astype(BF16)
    ckvn = _rmsnorm(lat[:, o_kv:o_kr], gckv_ref[...]).astype(BF16)
    kr_lin = lat[:, o_kr:o_kr + HEAD_PAD]
    kr_swp = lat[:, o_kr + HEAD_PAD:o_kr + 2 * HEAD_PAD]

    half = MLA_ROPE // 2
    ang = invf_ref[...] * pos_ref[0].astype(F32)
    cos_h = jnp.cos(ang)
    sin_h = jnp.sin(ang)
    cos_r = jnp.concatenate([cos_h, cos_h], axis=0)
    sin_r = jnp.concatenate([-sin_h, sin_h], axis=0)

    zeros_lo = jnp.zeros((MLA_NOPE, tm), F32)
    zeros_hi = jnp.zeros((HEAD_PAD - MLA_NOPE - MLA_ROPE, tm), F32)
    cos_n = jnp.concatenate([zeros_lo, cos_r, zeros_hi], axis=0).T
    sin_n = jnp.concatenate([zeros_lo, sin_r, zeros_hi], axis=0).T
    k_rope = kr_lin * cos_n + kr_swp * sin_n
    k_nope = _dot(ckvn, wukvk_ref[...])
    for hd in range(MLA_HEADS):
        km_ref[0, :, hd * HEAD_PAD:(hd + 1) * HEAD_PAD] = (
            k_nope[:, hd * HEAD_PAD:(hd + 1) * HEAD_PAD] + k_rope).astype(BF16)

    vt = _dot_nt(wukvvt_ref[...], ckvn)
    qt_lin = _dot_nt(wuqt_ref[...], cqn)
    qt_swp = _dot_nt(wuqswt_ref[...], cqn)
    qs_mla = (MLA_NOPE + MLA_ROPE) ** -0.5 * LOG2E
    pad_rows = jnp.zeros((HEAD_PAD - MLA_NOPE - MLA_ROPE, tm), F32)
    qt_heads = []
    for hd in range(MLA_HEADS):
        lin = qt_lin[hd * HEAD_PAD:(hd + 1) * HEAD_PAD]
        rot = (lin[MLA_NOPE:MLA_NOPE + MLA_ROPE] * cos_r
               + qt_swp[hd * MLA_ROPE:(hd + 1) * MLA_ROPE] * sin_r)
        qt_heads.append(jnp.concatenate([lin[0:MLA_NOPE], rot, pad_rows], axis=0))
    qt = jnp.concatenate(qt_heads, axis=0) * qs_mla
    for c in range(tm // tq):
        qtm_ref[0, c] = qt[:, c * tq:(c + 1) * tq].astype(BF16)
        vtm_ref[0, c] = vt[:, c * tq:(c + 1) * tq].astype(BF16)


def _in_proj(x, mod, positions3, invf, w, l, *, tm, tq):
    bsz, s, d = x.shape
    nt = s // tm
    nq = s // tq
    kern = functools.partial(_in_proj_kernel, tm=tm, tq=tq)
    names = ["g_pre_mix", None, None, "w_nat", "w_t", "g_cq", "wuq_t", "wuq_swp_t", "g_ckv",
             "wukv_k", "wukv_v_t", "sc_conv", "cf_conv", "cf_conv_b", "cf_ln_g", "cf_ln_b"]
    in_specs = [pl.BlockSpec((1, tm, d), lambda b, t: (b, t, 0)), _mod_spec(d, l)]
    args = [x, mod]
    fixed = iter([(positions3, pl.BlockSpec((1, 1, tm), lambda b, t: (b, 0, t))),
                  (invf, _const_spec(invf.shape))])
    for n in names:
        arr, spec = next(fixed) if n is None else (w[n], _layer_spec(w[n], l))
        in_specs.append(spec)
        args.append(arr)
    slab = tm // tq
    out_shape = (
        jax.ShapeDtypeStruct((bsz, nq, DA_WIDTH, tq), BF16),
        jax.ShapeDtypeStruct((bsz, s, DA_WIDTH), BF16),
        jax.ShapeDtypeStruct((bsz, nq, DA_WIDTH, tq), BF16),
        jax.ShapeDtypeStruct((bsz, nq, MLA_HEADS * HEAD_PAD, tq), BF16),
        jax.ShapeDtypeStruct((bsz, s, MLA_HEADS * HEAD_PAD), BF16),
        jax.ShapeDtypeStruct((bsz, nq, MLA_HEADS * MLA_V, tq), BF16),
        jax.ShapeDtypeStruct((bsz, s, SC_WIDTH), BF16),
        jax.ShapeDtypeStruct((bsz, s, CF_WIDTH), BF16),
    )

    def tspec(rows):
        return pl.BlockSpec((1, slab, rows, tq), lambda b, t: (b, t, 0, 0))

    def nspec(cols):
        return pl.BlockSpec((1, tm, cols), lambda b, t: (b, t, 0))

    out_specs = (tspec(DA_WIDTH), nspec(DA_WIDTH), tspec(DA_WIDTH),
                 tspec(MLA_HEADS * HEAD_PAD), nspec(MLA_HEADS * HEAD_PAD),
                 tspec(MLA_HEADS * MLA_V), nspec(SC_WIDTH), nspec(CF_WIDTH))
    return pl.pallas_call(
        kern,
        out_shape=out_shape,
        grid=(bsz, nt),
        in_specs=in_specs,
        out_specs=out_specs,
        scratch_shapes=[pltpu.VMEM((tm + SC_HALO, SC_WIDTH), F32),
                        pltpu.VMEM((tm + CF_HALO, CF_WIDTH), F32),
                        pltpu.VMEM((SUBLANES - 1, tm + CF_HALO, CF_WIDTH), F32)],
        compiler_params=_params("arbitrary", "arbitrary"),
        name="in_proj",
    )(*args)


def _attn_kernel(qd_ref, qdn_ref, kd_ref, vd_ref, qm_ref, qmn_ref, km_ref, vm_ref,
                 lq1_ref, lk1_ref, lq2_ref, lk2_ref, gsub_ref, od_ref, om_ref,
                 qcat_ref, m_ref, l_ref, acc_ref, s_ref, smax_ref, *, tq, lam_init):
    i = pl.program_id(1)
    tk = tq
    n_groups = 2 * ATTN_GROUPS

    for g in range(n_groups):
        m_ref[g] = jnp.full(m_ref.shape[1:], NEG_BIG, F32)
        l_ref[g] = jnp.zeros(l_ref.shape[1:], F32)
        acc_ref[g] = jnp.zeros(acc_ref.shape[1:], F32)

    def load_queries(src_ref):
        for g in range(ATTN_GROUPS):
            qt = src_ref[0, 0, g * HEAD_PAD:(g + 1) * HEAD_PAD, :]
            row = lax.broadcasted_iota(jnp.int32, qt.shape, 0)
            zero = jnp.zeros_like(qt)
            qcat_ref[g, :, 0:tq] = jnp.where(row < DA_HEAD_DIM, qt, zero)
            qcat_ref[g, :, tq:2 * tq] = jnp.where(row >= DA_HEAD_DIM, qt, zero)

    def scores(j, slot, src_ref=qm_ref):
        k0 = pl.multiple_of(j * tk, tk)
        for g in range(ATTN_GROUPS):
            c0 = g * HEAD_PAD
            s = _dot(kd_ref[0, pl.ds(k0, tk), c0:c0 + HEAD_PAD], qcat_ref[g])
            s_ref[slot, g] = s
            smax_ref[slot, g] = jnp.max(s, axis=0, keepdims=True)
        for g in range(ATTN_GROUPS, n_groups):
            for n in range(2):
                r0 = (2 * (g - ATTN_GROUPS) + n) * HEAD_PAD
                s = _dot(km_ref[0, pl.ds(k0, tk), r0:r0 + HEAD_PAD],
                         src_ref[0, 0, r0:r0 + HEAD_PAD, :])
                s_ref[slot, g, :, n * tq:(n + 1) * tq] = s
                smax_ref[slot, g, :, n * tq:(n + 1) * tq] = jnp.max(s, axis=0, keepdims=True)

    def softmax_pv(j, slot, masked):
        for g in range(n_groups):
            v_ref, v0 = (vd_ref, g * HEAD_PAD) if g < ATTN_GROUPS else (
                vm_ref, (g - ATTN_GROUPS) * HEAD_PAD)
            s = s_ref[slot, g]
            if masked:
                kpos = lax.broadcasted_iota(jnp.int32, (tk, tq), 0)
                qpos = lax.broadcasted_iota(jnp.int32, (tk, tq), 1)
                keep = kpos <= qpos
                keep = jnp.concatenate([keep, keep], axis=1)
                s = jnp.where(keep, s, NEG_BIG)
                smax = jnp.max(s, axis=0, keepdims=True)
            else:
                smax = smax_ref[slot, g]
            m_old = m_ref[g]
            m_new = jnp.maximum(m_old, smax)
            alpha = jnp.exp2(m_old - m_new)
            p = jnp.exp2(s - m_new).astype(BF16)
            m_ref[g] = m_new
            vt = jnp.concatenate([v_ref[0, j, v0:v0 + HEAD_PAD, :],
                                  jnp.ones((ONES_ROWS, tk), BF16)], axis=0)
            pv = _dot(vt, p)
            acc_ref[g] = alpha * acc_ref[g] + pv[0:HEAD_PAD]
            l_ref[g] = alpha * l_ref[g] + pv[HEAD_PAD:HEAD_PAD + 1]

    @pl.when(i == 0)
    def _():
        load_queries(qd_ref)
        scores(0, 0)

    def body(jj, carry):
        j = 2 * jj
        scores(j + 1, 1)
        softmax_pv(j, 0, False)
        scores(j + 2, 0)
        softmax_pv(j + 1, 1, False)
        return carry

    lax.fori_loop(0, i // 2, body, 0)

    @pl.when(i % 2 == 1)
    def _():
        scores(i, 1)
        softmax_pv(i - 1, 0, False)

    softmax_pv(i, i % 2, True)

    load_queries(qdn_ref)
    scores(0, 0, qmn_ref)

    lam = (jnp.exp(jnp.sum(lq1_ref[...] * lk1_ref[...], axis=-1, keepdims=True))
           - jnp.exp(jnp.sum(lq2_ref[...] * lk2_ref[...], axis=-1, keepdims=True)) + lam_init)
    for g in range(ATTN_GROUPS):
        acc = acc_ref[g]
        l = l_ref[g]
        o = acc[:, 0:tq] / l[:, 0:tq] - lam * (acc[:, tq:2 * tq] / l[:, tq:2 * tq])
        o = o * lax.rsqrt(jnp.mean(o * o, axis=0, keepdims=True) + NORM_EPS)
        o = o * gsub_ref[...] * (1.0 - lam_init)
        od_ref[0, :, g * HEAD_PAD:(g + 1) * HEAD_PAD] = o.T.astype(BF16)
    for g in range(ATTN_GROUPS):
        acc = acc_ref[ATTN_GROUPS + g]
        l = l_ref[ATTN_GROUPS + g]
        o = jnp.concatenate([acc[0:MLA_V, 0:tq] / l[:, 0:tq],
                             acc[MLA_V:2 * MLA_V, tq:2 * tq] / l[:, tq:2 * tq]], axis=0)
        om_ref[0, :, g * HEAD_PAD:(g + 1) * HEAD_PAD] = o.T.astype(BF16)


def _attention(qt_da, k_da, vt_da, qt_m, k_m, vt_m, extra, extra_specs, *, tq, lam_init):
    bsz, nq, _, _ = qt_da.shape
    s = k_da.shape[1]
    width = ATTN_GROUPS * HEAD_PAD
    n_groups = 2 * ATTN_GROUPS
    kern = functools.partial(_attn_kernel, tq=tq, lam_init=lam_init)
    in_specs, args = [], []
    for qt, k, vt in ((qt_da, k_da, vt_da), (qt_m, k_m, vt_m)):
        rows = qt.shape[2]
        in_specs += [
            pl.BlockSpec((1, 1, rows, tq), lambda b, i: (b, i, 0, 0)),
            pl.BlockSpec((1, 1, rows, tq), lambda b, i: (b, jnp.minimum(i + 1, nq - 1), 0, 0)),
            pl.BlockSpec((1, s, rows), lambda b, i: (b, 0, 0)),
            pl.BlockSpec((1, nq, width, tq), lambda b, i: (b, 0, 0, 0)),
        ]
        args += [qt, qt, k, vt]
    in_specs += extra_specs
    out = jax.ShapeDtypeStruct((bsz, s, width), BF16)
    out_spec = pl.BlockSpec((1, tq, width), lambda b, i: (b, i, 0))
    return pl.pallas_call(
        kern,
        out_shape=(out, out),
        grid=(bsz, nq),
        in_specs=in_specs,
        out_specs=(out_spec, out_spec),
        scratch_shapes=[pltpu.VMEM((ATTN_GROUPS, HEAD_PAD, 2 * tq), BF16),
                        pltpu.VMEM((n_groups, 1, 2 * tq), F32),
                        pltpu.VMEM((n_groups, 1, 2 * tq), F32),
                        pltpu.VMEM((n_groups, HEAD_PAD, 2 * tq), F32),
                        pltpu.VMEM((2, n_groups, tq, 2 * tq), F32),
                        pltpu.VMEM((2, n_groups, 1, 2 * tq), F32)],
        compiler_params=_params("arbitrary", "arbitrary"),
        name="causal_attention",
    )(*args, *extra)


def _merge_kernel(x_ref, mod_ref, gpre_ref, wg_ref, ya_ref, yb_ref, yc_ref, yd_ref,
                  woa_ref, wob_ref, woc_ref, wod_ref, wmix_ref, gpost_ref, o_ref):
    x = x_ref[0]
    shift = mod_ref[0, 0:1, :]
    scale = mod_ref[0, 1:2, :]
    gate = mod_ref[0, 2:3, :]
    h = (_rmsnorm(x, gpre_ref[...]) * (1.0 + scale) + shift).astype(BF16)
    merged = None
    branches = ((ya_ref, woa_ref), (yb_ref, wob_ref), (yc_ref, woc_ref), (yd_ref, wod_ref))
    for n, (pre_ref, wo_ref) in enumerate(branches):
        g = _sigmoid(_dot(h, wg_ref[:, n * D_MODEL:(n + 1) * D_MODEL]))
        term = g * _dot(pre_ref[0], wo_ref[...])
        merged = term if merged is None else merged + term
    z = _dot(merged.astype(BF16), wmix_ref[...])
    o_ref[0] = x + gate * _rmsnorm(z, gpost_ref[...])


def _merge(x, mod, ya, yb, yc, yd, w, l, *, tm):
    bsz, s, d = x.shape

    def tok(cols):
        return pl.BlockSpec((1, tm, cols), lambda b, t: (b, t, 0))

    consts_a = [w["g_pre_mix"], w["w_gates"]]
    consts_b = [w["da_w_o"], w["mla_w_o"], w["sc_w_o"], w["cf_w_o"], w["w_mix_out"],
                w["g_post_mix"]]
    in_specs = ([tok(d), _mod_spec(d, l)]
                + [_layer_spec(a, l) for a in consts_a]
                + [tok(a.shape[-1]) for a in (ya, yb, yc, yd)]
                + [_layer_spec(a, l) for a in consts_b])
    return pl.pallas_call(
        _merge_kernel,
        out_shape=jax.ShapeDtypeStruct(x.shape, F32),
        grid=(bsz, s // tm),
        in_specs=in_specs,
        out_specs=tok(d),
        compiler_params=_params("arbitrary", "arbitrary"),
        name="merge_out_proj",
    )(x, mod, *consts_a, ya, yb, yc, yd, *consts_b)


def _ffn_kernel(x_ref, mod_ref, gpre_ref, wup_ref, conv_ref, wdown_ref, gpost_ref, o_ref,
                abuf, carry, act_ref, *, tm):
    t = pl.program_id(1)

    @pl.when(t == 0)
    def _():
        carry[...] = jnp.zeros(carry.shape, F32)

    x = x_ref[0]
    shift = mod_ref[0, 3:4, :]
    scale = mod_ref[0, 4:5, :]
    gate = mod_ref[0, 5:6, :]
    h = (_rmsnorm(x, gpre_ref[...]) * (1.0 + scale) + shift).astype(BF16)
    fc = FFN_COL_CHUNK
    for c0 in range(0, FFN_DIM, fc):
        a = _dot(h, wup_ref[:, c0:c0 + fc])
        b = _dot(h, wup_ref[:, FFN_DIM + c0:FFN_DIM + c0 + fc])
        abuf[0:FFN_HALO, :] = carry[:, c0:c0 + fc]
        abuf[FFN_HALO:FFN_HALO + tm, :] = a
        carry[:, c0:c0 + fc] = a[tm - FFN_HALO:tm, :]
        cv = (conv_ref[2:3, c0:c0 + fc] * a
              + conv_ref[1:2, c0:c0 + fc] * abuf[FFN_HALO - 1:FFN_HALO - 1 + tm, :]
              + conv_ref[0:1, c0:c0 + fc] * abuf[FFN_HALO - 2:FFN_HALO - 2 + tm, :])
        act_ref[:, c0:c0 + fc] = (cv * _sigmoid(cv) * b).astype(BF16)
    y = _dot(act_ref[...], wdown_ref[...])
    o_ref[0] = x + gate * _rmsnorm(y, gpost_ref[...])


def _ffn(x, mod, w, l, *, tm):
    bsz, s, d = x.shape
    tok = pl.BlockSpec((1, tm, d), lambda b, t: (b, t, 0))
    consts = [w["g_pre_ffn"], w["w_up"], w["ffn_conv"], w["w_down"], w["g_post_ffn"]]
    return pl.pallas_call(
        functools.partial(_ffn_kernel, tm=tm),
        out_shape=jax.ShapeDtypeStruct(x.shape, F32),
        grid=(bsz, s // tm),
        in_specs=[tok, _mod_spec(d, l)] + [_layer_spec(a, l) for a in consts],
        out_specs=tok,
        scratch_shapes=[pltpu.VMEM((tm + FFN_HALO, FFN_COL_CHUNK), F32),
                        pltpu.VMEM((FFN_HALO, FFN_DIM), F32),
                        pltpu.VMEM((tm, FFN_DIM), BF16)],
        compiler_params=_params("arbitrary", "arbitrary"),
        name="gated_mlp",
    )(x, mod, *consts)


def _split_w_in(w_in):
    sizes = (DA_WIDTH, DA_WIDTH, DA_WIDTH, MLA_Q_LORA, MLA_KV_LORA, MLA_ROPE,
             SC_WIDTH, SC_WIDTH, SC_WIDTH, 2 * CF_WIDTH, N_BRANCHES * D_MODEL)
    out, acc = [], 0
    for sz in sizes:
        out.append(w_in[..., acc:acc + sz])
        acc += sz
    return out


def _prep_weights(p):
    depth = p["w_in"].shape[0]
    row = lambda a: a.reshape(depth, 1, -1)
    tr = lambda a: jnp.swapaxes(a, 1, 2)
    (w_q, w_k, w_v, w_cq, w_ckv, w_kr, w_u, w_b, w_c, w_cf, w_g) = _split_w_in(p["w_in"])
    d = w_q.shape[1]
    half = MLA_ROPE // 2
    pad_hi = HEAD_PAD - MLA_NOPE - MLA_ROPE
    z = lambda n: jnp.zeros((depth, d, n), F32)
    kr_lin = jnp.concatenate([z(MLA_NOPE), w_kr, z(pad_hi)], axis=2)
    kr_swp = jnp.concatenate([z(MLA_NOPE), w_kr[..., half:], w_kr[..., :half], z(pad_hi)], axis=2)
    w_nat = jnp.concatenate([w_u, w_b, w_c, w_cf, w_k, w_cq, w_ckv, kr_lin, kr_swp], axis=2)
    w_t = tr(jnp.concatenate([w_q, w_v], axis=2))

    w_uq = p["mla_w_uq"].reshape(depth, MLA_Q_LORA, MLA_HEADS, MLA_NOPE + MLA_ROPE)
    uq_pad = jnp.pad(w_uq, ((0, 0), (0, 0), (0, 0), (0, pad_hi)))
    rope = w_uq[..., MLA_NOPE:]
    uq_swp = jnp.concatenate([rope[..., half:], rope[..., :half]], axis=3)
    w_ukv = p["mla_w_ukv"].reshape(depth, MLA_KV_LORA, MLA_HEADS, MLA_NOPE + MLA_V)
    ukv_k = jnp.pad(w_ukv[..., :MLA_NOPE], ((0, 0), (0, 0), (0, 0), (0, HEAD_PAD - MLA_NOPE)))
    ukv_v = w_ukv[..., MLA_NOPE:]
    return {
        "g_pre_mix": row(p["g_pre_mix"]),
        "w_nat": w_nat.astype(BF16),
        "w_t": w_t.astype(BF16),
        "w_gates": w_g.astype(BF16),
        "g_cq": row(p["mla_g_cq"]),
        "wuq_t": tr(uq_pad.reshape(depth, MLA_Q_LORA, -1)).astype(BF16),
        "wuq_swp_t": tr(uq_swp.reshape(depth, MLA_Q_LORA, -1)).astype(BF16),
        "g_ckv": row(p["mla_g_ckv"]),
        "wukv_k": ukv_k.reshape(depth, MLA_KV_LORA, -1).astype(BF16),
        "wukv_v_t": tr(ukv_v.reshape(depth, MLA_KV_LORA, -1)).astype(BF16),
        "sc_conv": p["sc_conv"],
        "cf_conv": p["cf_conv"],
        "cf_conv_b": row(p["cf_conv_b"]),
        "cf_ln_g": row(p["cf_ln_g"]),
        "cf_ln_b": row(p["cf_ln_b"]),
        "lam": [row(p[n]) for n in ("da_lam_q1", "da_lam_k1", "da_lam_q2", "da_lam_k2")],
        "g_subln": p["da_g_subln"].reshape(depth, -1, 1),
        "da_w_o": p["da_w_o"].astype(BF16),
        "mla_w_o": p["mla_w_o"].astype(BF16),
        "sc_w_o": p["sc_w_o"].astype(BF16),
        "cf_w_o": p["cf_w_o"].astype(BF16),
        "w_mix_out": p["w_mix_out"].astype(BF16),
        "g_post_mix": row(p["g_post_mix"]),
        "g_pre_ffn": row(p["g_pre_ffn"]),
        "w_up": p["w_up"].astype(BF16),
        "ffn_conv": p["ffn_conv"],
        "w_down": p["w_down"].astype(BF16),
        "g_post_ffn": row(p["g_post_ffn"]),
    }


def kernel(x, c, positions, w_ada, b_ada, g_pre_mix, w_in, da_lam_q1, da_lam_k1, da_lam_q2, da_lam_k2, da_g_subln, da_w_o, mla_g_cq, mla_w_uq, mla_g_ckv, mla_w_ukv, mla_w_o, sc_conv, sc_w_o, cf_conv, cf_conv_b, cf_ln_g, cf_ln_b, cf_w_o, w_mix_out, g_post_mix, g_pre_ffn, w_up, ffn_conv, w_down, g_post_ffn):
    p = dict(g_pre_mix=g_pre_mix, w_in=w_in, da_lam_q1=da_lam_q1, da_lam_k1=da_lam_k1,
             da_lam_q2=da_lam_q2, da_lam_k2=da_lam_k2, da_g_subln=da_g_subln, da_w_o=da_w_o,
             mla_g_cq=mla_g_cq, mla_w_uq=mla_w_uq, mla_g_ckv=mla_g_ckv, mla_w_ukv=mla_w_ukv,
             mla_w_o=mla_w_o, sc_conv=sc_conv, sc_w_o=sc_w_o, cf_conv=cf_conv,
             cf_conv_b=cf_conv_b, cf_ln_g=cf_ln_g, cf_ln_b=cf_ln_b, cf_w_o=cf_w_o,
             w_mix_out=w_mix_out, g_post_mix=g_post_mix, g_pre_ffn=g_pre_ffn, w_up=w_up,
             ffn_conv=ffn_conv, w_down=w_down, g_post_ffn=g_post_ffn)
    bsz, s, _ = x.shape
    depth = w_ada.shape[0]
    tm = min(TOKEN_TILE, s)
    tm_mix = min(MIX_TOKEN_TILE, s)
    tq = min(ATTN_TILE, s)
    assert s % tm == 0 and tm % tq == 0 and s % tm_mix == 0

    half = MLA_ROPE // 2
    inv_freq = ROPE_THETA ** (-jnp.arange(half, dtype=F32) / half)
    invf = jnp.broadcast_to(inv_freq[:, None], (half, tm))
    positions3 = positions.reshape(bsz, 1, s)

    mod = _ada_modulation(c, w_ada, b_ada)
    w = _prep_weights(p)
    for l in range(depth):
        lam_init = 0.8 - 0.6 * math.exp(-0.3 * l)
        qt_da, k_da, vt_da, qt_m, k_m, vt_m, yc, yd = _in_proj(
            x, mod, positions3, invf, w, l, tm=tm, tq=tq)
        da_extra = w["lam"] + [w["g_subln"]]
        ya, yb = _attention(qt_da, k_da, vt_da, qt_m, k_m, vt_m, da_extra,
                            [_layer_spec(a, l) for a in da_extra], tq=tq, lam_init=lam_init)
        x = _merge(x, mod, ya, yb, yc, yd, w, l, tm=tm_mix)
        x = _ffn(x, mod, w, l, tm=tm_mix)
    return x
```

```python
import functools
import math

import jax
import jax.numpy as jnp
from jax import lax
from jax.experimental import pallas as pl
from jax.experimental.pallas import tpu as pltpu

D_MODEL = 1024
DA_HEADS = 4
DA_HEAD_DIM = 64
DA_WIDTH = DA_HEADS * 2 * DA_HEAD_DIM
MLA_HEADS = 8
MLA_Q_LORA = 384
MLA_KV_LORA = 256
MLA_NOPE = 64
MLA_ROPE = 32
MLA_V = 64
ROPE_THETA = 10000.0
SC_WIDTH = 512
SC_KERNEL = 3
CF_WIDTH = 512
CF_KERNEL = 31
N_BRANCHES = 4
FFN_DIM = 2816
FFN_KERNEL = 3
N_MOD = 6
NORM_EPS = 1e-6
LN_EPS = 1e-5

HEAD_PAD = 128
LOG2E = math.log2(math.e)
NEG_BIG = -1e30

VMEM_LIMIT_BYTES = 56 * 1024 * 1024
TOKEN_TILE = 512
MIX_TOKEN_TILE = 1024
ATTN_TILE = 256
SUBLANES = 8
CF_ROW_CHUNK = 64
FFN_COL_CHUNK = 256
SC_HALO = 8
CF_HALO = 32
FFN_HALO = 8
ONES_ROWS = 16
ATTN_GROUPS = 4

BF16 = jnp.bfloat16
F32 = jnp.float32


def _dot(a, b):
    return jnp.dot(a, b, preferred_element_type=F32)


def _dot_nt(a, b):
    return lax.dot_general(a, b, (((1,), (1,)), ((), ())), preferred_element_type=F32)


def _sigmoid(x):
    return 1.0 / (1.0 + jnp.exp(-x))


def _rmsnorm(x, g):
    return x * lax.rsqrt(jnp.mean(x * x, axis=-1, keepdims=True) + NORM_EPS) * g


def _const_spec(shape):
    nd = len(shape)
    return pl.BlockSpec(shape, lambda *_: (0,) * nd, pipeline_mode=pl.Buffered(1))


def _layer_spec(arr, l):
    nd = arr.ndim
    return pl.BlockSpec((None,) + arr.shape[1:], lambda *_: (l,) + (0,) * (nd - 1),
                        pipeline_mode=pl.Buffered(1))


def _mod_spec(d, l):
    return pl.BlockSpec((None, 1, N_MOD, d), lambda b, t: (l, b, 0, 0))


def _params(*sem):
    return pltpu.CompilerParams(dimension_semantics=sem, vmem_limit_bytes=VMEM_LIMIT_BYTES)


def _ada_kernel(c_ref, w_ref, b_ref, o_ref):
    c = c_ref[...]
    act = (c * _sigmoid(c)).astype(BF16)
    o_ref[0] = _dot(act, w_ref[0].astype(BF16)) + b_ref[0]


def _ada_modulation(c, w_ada, b_ada):
    depth, d, n = w_ada.shape
    bsz = c.shape[0]
    out = pl.pallas_call(
        _ada_kernel,
        out_shape=jax.ShapeDtypeStruct((depth, bsz, n), F32),
        grid=(depth, n // d),
        in_specs=[
            pl.BlockSpec((bsz, d), lambda l, j: (0, 0)),
            pl.BlockSpec((1, d, d), lambda l, j: (l, 0, j)),
            pl.BlockSpec((1, 1, d), lambda l, j: (l, 0, j)),
        ],
        out_specs=pl.BlockSpec((1, bsz, d), lambda l, j: (l, 0, j)),
        compiler_params=_params("arbitrary", "arbitrary"),
        name="ada_modulation",
    )(c, w_ada, b_ada.reshape(depth, 1, n))
    return out.reshape(depth, bsz, N_MOD, d)


_C_SC = 0
_C_CF = _C_SC + 3 * SC_WIDTH
_C_DAK = _C_CF + 2 * CF_WIDTH
_C_CQ = _C_DAK + DA_WIDTH
_C_END = _C_CQ + MLA_Q_LORA + MLA_KV_LORA + 2 * HEAD_PAD


def _in_proj_kernel(x_ref, mod_ref, gpre_ref, pos_ref, invf_ref, wnat_ref, wt_ref,
                    gcq_ref, wuqt_ref, wuqswt_ref, gckv_ref, wukvk_ref, wukvvt_ref,
                    scconv_ref, cfconv_ref, cfb_ref, lng_ref, lnb_ref,
                    qtda_ref, kda_ref, vtda_ref, qtm_ref, km_ref, vtm_ref, yc_ref, yd_ref,
                    wbuf, ubuf, ushift, *, tm, tq):
    t = pl.program_id(1)

    @pl.when(t == 0)
    def _():
        wbuf[0:SC_HALO, :] = jnp.zeros((SC_HALO, SC_WIDTH), F32)
        ubuf[0:CF_HALO, :] = jnp.zeros((CF_HALO, CF_WIDTH), F32)

    x = x_ref[0]
    shift = mod_ref[0, 0:1, :]
    scale = mod_ref[0, 1:2, :]
    h = (_rmsnorm(x, gpre_ref[...]) * (1.0 + scale) + shift).astype(BF16)

    cf = _dot(h, wnat_ref[:, _C_CF:_C_DAK])
    ubuf[CF_HALO:CF_HALO + tm, :] = cf[:, 0:CF_WIDTH] * _sigmoid(cf[:, CF_WIDTH:2 * CF_WIDTH])

    def conv_work():
        for b in range(1, SUBLANES):
            ushift[b - 1] = pltpu.roll(ubuf[...], b, axis=0)
            yield
        for r0 in range(0, tm, CF_ROW_CHUNK):
            acc = jnp.broadcast_to(cfb_ref[...], (CF_ROW_CHUNK, CF_WIDTH))
            for s in range(CF_KERNEL):
                a, b = divmod(s, SUBLANES)
                lo = CF_HALO + r0 - SUBLANES * a
                src = (ubuf[lo:lo + CF_ROW_CHUNK, :] if b == 0
                       else ushift[b - 1, lo:lo + CF_ROW_CHUNK, :])
                acc = acc + cfconv_ref[CF_KERNEL - 1 - s:CF_KERNEL - s, :] * src
            mu = jnp.mean(acc, axis=-1, keepdims=True)
            cen = acc - mu
            var = jnp.mean(cen * cen, axis=-1, keepdims=True)
            y = cen * lax.rsqrt(var + LN_EPS) * lng_ref[...] + lnb_ref[...]
            yd_ref[0, r0:r0 + CF_ROW_CHUNK, :] = (y * _sigmoid(y)).astype(BF16)
            yield
        ubuf[0:CF_HALO, :] = ubuf[tm:tm + CF_HALO, :]

    conv_units = conv_work()

    def advance(n):
        for _ in range(n):
            next(conv_units, None)

    sc_u = _dot(h, wnat_ref[:, _C_SC:_C_SC + SC_WIDTH])
    advance(2)
    sc_c = _dot(h, wnat_ref[:, _C_SC + 2 * SC_WIDTH:_C_CF])
    w = sc_c * sc_u
    wbuf[SC_HALO:SC_HALO + tm, :] = w
    advance(2)
    sc_b = _dot(h, wnat_ref[:, _C_SC + SC_WIDTH:_C_SC + 2 * SC_WIDTH])
    conv = (scconv_ref[2:3, :] * w
            + scconv_ref[1:2, :] * wbuf[SC_HALO - 1:SC_HALO - 1 + tm, :]
            + scconv_ref[0:1, :] * wbuf[SC_HALO - 2:SC_HALO - 2 + tm, :])
    yc_ref[0] = (sc_b * conv).astype(BF16)
    wbuf[0:SC_HALO, :] = wbuf[tm:tm + SC_HALO, :]
    advance(1)

    kda_ref[0] = _dot(h, wnat_ref[:, _C_DAK:_C_CQ]).astype(BF16)
    advance(2)
    qs_da = DA_HEAD_DIM ** -0.5 * LOG2E
    tr_q = _dot_nt(wt_ref[0:DA_WIDTH, :], h)
    for c in range(tm // tq):
        qtda_ref[0, c] = (tr_q[:, c * tq:(c + 1) * tq] * qs_da).astype(BF16)
    advance(1)
    tr_v = _dot_nt(wt_ref[DA_WIDTH:2 * DA_WIDTH, :], h)
    for c in range(tm // tq):
        vtda_ref[0, c] = tr_v[:, c * tq:(c + 1) * tq].astype(BF16)
    advance(1)

    lat = _dot(h, wnat_ref[:, _C_CQ:_C_END])
    o_kv = MLA_Q_LORA
    o_kr = o_kv + MLA_KV_LORA
    cqn = _rmsnorm(lat[:, 0:o_kv], gcq_ref[...]).astype(BF16)
    ckvn = _rmsnorm(lat[:, o_kv:o_kr], gckv_ref[...]).astype(BF16)
    kr_lin = lat[:, o_kr:o_kr + HEAD_PAD]
    kr_swp = lat[:, o_kr + HEAD_PAD:o_kr + 2 * HEAD_PAD]
    advance(1)

    half = MLA_ROPE // 2
    ang = invf_ref[...] * pos_ref[0].astype(F32)
    cos_h = jnp.cos(ang)
    sin_h = jnp.sin(ang)
    cos_r = jnp.concatenate([cos_h, cos_h], axis=0)
    sin_r = jnp.concatenate([-sin_h, sin_h], axis=0)

    zeros_lo = jnp.zeros((MLA_NOPE, tm), F32)
    zeros_hi = jnp.zeros((HEAD_PAD - MLA_NOPE - MLA_ROPE, tm), F32)
    cos_n = jnp.concatenate([zeros_lo, cos_r, zeros_hi], axis=0).T
    sin_n = jnp.concatenate([zeros_lo, sin_r, zeros_hi], axis=0).T
    k_rope = kr_lin * cos_n + kr_swp * sin_n
    advance(1)
    k_nope = _dot(ckvn, wukvk_ref[...])
    for hd in range(MLA_HEADS):
        km_ref[0, :, hd * HEAD_PAD:(hd + 1) * HEAD_PAD] = (
            k_nope[:, hd * HEAD_PAD:(hd + 1) * HEAD_PAD] + k_rope).astype(BF16)
    advance(2)

    vt = _dot_nt(wukvvt_ref[...], ckvn)
    for c in range(tm // tq):
        vtm_ref[0, c] = vt[:, c * tq:(c + 1) * tq].astype(BF16)
    advance(1)
    qt_lin = _dot_nt(wuqt_ref[...], cqn)
    advance(1)
    qt_swp = _dot_nt(wuqswt_ref[...], cqn)
    qs_mla = (MLA_NOPE + MLA_ROPE) ** -0.5 * LOG2E
    pad_rows = jnp.zeros((HEAD_PAD - MLA_NOPE - MLA_ROPE, tm), F32)
    qt_heads = []
    for hd in range(MLA_HEADS):
        lin = qt_lin[hd * HEAD_PAD:(hd + 1) * HEAD_PAD]
        rot = (lin[MLA_NOPE:MLA_NOPE + MLA_ROPE] * cos_r
               + qt_swp[hd * MLA_ROPE:(hd + 1) * MLA_ROPE] * sin_r)
        qt_heads.append(jnp.concatenate([lin[0:MLA_NOPE], rot, pad_rows], axis=0))
    qt = jnp.concatenate(qt_heads, axis=0) * qs_mla
    for c in range(tm // tq):
        qtm_ref[0, c] = qt[:, c * tq:(c + 1) * tq].astype(BF16)
    advance(CF_KERNEL)


def _in_proj(x, mod, positions3, invf, w, l, *, tm, tq):
    bsz, s, d = x.shape
    nt = s // tm
    nq = s // tq
    kern = functools.partial(_in_proj_kernel, tm=tm, tq=tq)
    names = ["g_pre_mix", None, None, "w_nat", "w_t", "g_cq", "wuq_t", "wuq_swp_t", "g_ckv",
             "wukv_k", "wukv_v_t", "sc_conv", "cf_conv", "cf_conv_b", "cf_ln_g", "cf_ln_b"]
    in_specs = [pl.BlockSpec((1, tm, d), lambda b, t: (b, t, 0)), _mod_spec(d, l)]
    args = [x, mod]
    fixed = iter([(positions3, pl.BlockSpec((1, 1, tm), lambda b, t: (b, 0, t))),
                  (invf, _const_spec(invf.shape))])
    for n in names:
        arr, spec = next(fixed) if n is None else (w[n], _layer_spec(w[n], l))
        in_specs.append(spec)
        args.append(arr)
    slab = tm // tq
    out_shape = (
        jax.ShapeDtypeStruct((bsz, nq, DA_WIDTH, tq), BF16),
        jax.ShapeDtypeStruct((bsz, s, DA_WIDTH), BF16),
        jax.ShapeDtypeStruct((bsz, nq, DA_WIDTH, tq), BF16),
        jax.ShapeDtypeStruct((bsz, nq, MLA_HEADS * HEAD_PAD, tq), BF16),
        jax.ShapeDtypeStruct((bsz, s, MLA_HEADS * HEAD_PAD), BF16),
        jax.ShapeDtypeStruct((bsz, nq, MLA_HEADS * MLA_V, tq), BF16),
        jax.ShapeDtypeStruct((bsz, s, SC_WIDTH), BF16),
        jax.ShapeDtypeStruct((bsz, s, CF_WIDTH), BF16),
    )

    def tspec(rows):
        return pl.BlockSpec((1, slab, rows, tq), lambda b, t: (b, t, 0, 0))

    def nspec(cols):
        return pl.BlockSpec((1, tm, cols), lambda b, t: (b, t, 0))

    out_specs = (tspec(DA_WIDTH), nspec(DA_WIDTH), tspec(DA_WIDTH),
                 tspec(MLA_HEADS * HEAD_PAD), nspec(MLA_HEADS * HEAD_PAD),
                 tspec(MLA_HEADS * MLA_V), nspec(SC_WIDTH), nspec(CF_WIDTH))
    return pl.pallas_call(
        kern,
        out_shape=out_shape,
        grid=(bsz, nt),
        in_specs=in_specs,
        out_specs=out_specs,
        scratch_shapes=[pltpu.VMEM((tm + SC_HALO, SC_WIDTH), F32),
                        pltpu.VMEM((tm + CF_HALO, CF_WIDTH), F32),
                        pltpu.VMEM((SUBLANES - 1, tm + CF_HALO, CF_WIDTH), F32)],
        compiler_params=_params("arbitrary", "arbitrary"),
        name="in_proj",
    )(*args)


def _attn_kernel(qd_ref, qdn_ref, kd_ref, vd_ref, qm_ref, qmn_ref, km_ref, vm_ref,
                 lq1_ref, lk1_ref, lq2_ref, lk2_ref, gsub_ref, od_ref, om_ref,
                 qcat_ref, m_ref, l_ref, acc_ref, s_ref, smax_ref, *, tq, lam_init):
    i = pl.program_id(1)
    tk = tq
    n_groups = 2 * ATTN_GROUPS

    for g in range(n_groups):
        m_ref[g] = jnp.full(m_ref.shape[1:], NEG_BIG, F32)
        l_ref[g] = jnp.zeros(l_ref.shape[1:], F32)
        acc_ref[g] = jnp.zeros(acc_ref.shape[1:], F32)

    def load_queries(src_ref, buf):
        for g in range(ATTN_GROUPS):
            qt = src_ref[0, 0, g * HEAD_PAD:(g + 1) * HEAD_PAD, :]
            row = lax.broadcasted_iota(jnp.int32, qt.shape, 0)
            zero = jnp.zeros_like(qt)
            qcat_ref[buf, g, :, 0:tq] = jnp.where(row < DA_HEAD_DIM, qt, zero)
            qcat_ref[buf, g, :, tq:2 * tq] = jnp.where(row >= DA_HEAD_DIM, qt, zero)

    all_groups = tuple(range(n_groups))

    def scores(j, slot, qbuf, src_ref=qm_ref, groups=all_groups):
        k0 = pl.multiple_of(j * tk, tk)
        for g in groups:
            if g < ATTN_GROUPS:
                c0 = g * HEAD_PAD
                s = _dot(kd_ref[0, pl.ds(k0, tk), c0:c0 + HEAD_PAD], qcat_ref[qbuf, g])
                s_ref[slot, g] = s
                smax_ref[slot, g] = jnp.max(s, axis=0, keepdims=True)
                continue
            for n in range(2):
                r0 = (2 * (g - ATTN_GROUPS) + n) * HEAD_PAD
                s = _dot(km_ref[0, pl.ds(k0, tk), r0:r0 + HEAD_PAD],
                         src_ref[0, 0, r0:r0 + HEAD_PAD, :])
                s_ref[slot, g, :, n * tq:(n + 1) * tq] = s
                smax_ref[slot, g, :, n * tq:(n + 1) * tq] = jnp.max(s, axis=0, keepdims=True)

    def softmax_pv(j, slot, masked, groups=all_groups):
        for g in groups:
            v_ref, v0 = (vd_ref, g * HEAD_PAD) if g < ATTN_GROUPS else (
                vm_ref, (g - ATTN_GROUPS) * HEAD_PAD)
            s = s_ref[slot, g]
            if masked:
                kpos = lax.broadcasted_iota(jnp.int32, (tk, tq), 0)
                qpos = lax.broadcasted_iota(jnp.int32, (tk, tq), 1)
                keep = kpos <= qpos
                keep = jnp.concatenate([keep, keep], axis=1)
                s = jnp.where(keep, s, NEG_BIG)
                smax = jnp.max(s, axis=0, keepdims=True)
            else:
                smax = smax_ref[slot, g]
            m_old = m_ref[g]
            m_new = jnp.maximum(m_old, smax)
            alpha = jnp.exp2(m_old - m_new)
            p = jnp.exp2(s - m_new).astype(BF16)
            m_ref[g] = m_new
            vt = jnp.concatenate([v_ref[0, j, v0:v0 + HEAD_PAD, :],
                                  jnp.ones((ONES_ROWS, tk), BF16)], axis=0)
            pv = _dot(vt, p)
            acc_ref[g] = alpha * acc_ref[g] + pv[0:HEAD_PAD]
            l_ref[g] = alpha * l_ref[g] + pv[HEAD_PAD:HEAD_PAD + 1]

    @pl.when(i == 0)
    def _():
        load_queries(qd_ref, 0)
        scores(0, 0, 0)

    def body(jj, carry):
        j = 2 * jj
        for g in all_groups:
            scores(j + 1, 1, i % 2, groups=(g,))
            softmax_pv(j, 0, False, groups=(g,))
        for g in all_groups:
            scores(j + 2, 0, i % 2, groups=(g,))
            softmax_pv(j + 1, 1, False, groups=(g,))
        return carry

    lax.fori_loop(0, i // 2, body, 0)

    def finalize(g):
        acc = acc_ref[g]
        l = l_ref[g]
        if g < ATTN_GROUPS:
            lam = (jnp.exp(jnp.sum(lq1_ref[...] * lk1_ref[...], axis=-1, keepdims=True))
                   - jnp.exp(jnp.sum(lq2_ref[...] * lk2_ref[...], axis=-1, keepdims=True))
                   + lam_init)
            o = acc[:, 0:tq] / l[:, 0:tq] - lam * (acc[:, tq:2 * tq] / l[:, tq:2 * tq])
            o = o * lax.rsqrt(jnp.mean(o * o, axis=0, keepdims=True) + NORM_EPS)
            o = o * gsub_ref[...] * (1.0 - lam_init)
            od_ref[0, :, g * HEAD_PAD:(g + 1) * HEAD_PAD] = o.T.astype(BF16)
        else:
            o = jnp.concatenate([acc[0:MLA_V, 0:tq] / l[:, 0:tq],
                                 acc[MLA_V:2 * MLA_V, tq:2 * tq] / l[:, tq:2 * tq]], axis=0)
            c0 = (g - ATTN_GROUPS) * HEAD_PAD
            om_ref[0, :, c0:c0 + HEAD_PAD] = o.T.astype(BF16)

    def epilogue(slot):
        load_queries(qdn_ref, 1 - slot)
        for g in all_groups:
            softmax_pv(i, slot, True, groups=(g,))
            scores(0, 0, 1 - slot, qmn_ref, groups=(g,))
            finalize(g)

    @pl.when(i % 2 == 0)
    def _():
        epilogue(0)

    @pl.when(i % 2 == 1)
    def _():
        for g in all_groups:
            scores(i, 1, 1, groups=(g,))
            softmax_pv(i - 1, 0, False, groups=(g,))
        epilogue(1)


def _attention(qt_da, k_da, vt_da, qt_m, k_m, vt_m, extra, extra_specs, *, tq, lam_init):
    bsz, nq, _, _ = qt_da.shape
    s = k_da.shape[1]
    width = ATTN_GROUPS * HEAD_PAD
    n_groups = 2 * ATTN_GROUPS
    kern = functools.partial(_attn_kernel, tq=tq, lam_init=lam_init)
    in_specs, args = [], []
    for qt, k, vt in ((qt_da, k_da, vt_da), (qt_m, k_m, vt_m)):
        rows = qt.shape[2]
        in_specs += [
            pl.BlockSpec((1, 1, rows, tq), lambda b, i: (b, i, 0, 0)),
            pl.BlockSpec((1, 1, rows, tq), lambda b, i: (b, jnp.minimum(i + 1, nq - 1), 0, 0)),
            pl.BlockSpec((1, s, rows), lambda b, i: (b, 0, 0)),
            pl.BlockSpec((1, nq, width, tq), lambda b, i: (b, 0, 0, 0)),
        ]
        args += [qt, qt, k, vt]
    in_specs += extra_specs
    out = jax.ShapeDtypeStruct((bsz, s, width), BF16)
    out_spec = pl.BlockSpec((1, tq, width), lambda b, i: (b, i, 0))
    return pl.pallas_call(
        kern,
        out_shape=(out, out),
        grid=(bsz, nq),
        in_specs=in_specs,
        out_specs=(out_spec, out_spec),
        scratch_shapes=[pltpu.VMEM((2, ATTN_GROUPS, HEAD_PAD, 2 * tq), BF16),
                        pltpu.VMEM((n_groups, 1, 2 * tq), F32),
                        pltpu.VMEM((n_groups, 1, 2 * tq), F32),
                        pltpu.VMEM((n_groups, HEAD_PAD, 2 * tq), F32),
                        pltpu.VMEM((2, n_groups, tq, 2 * tq), F32),
                        pltpu.VMEM((2, n_groups, 1, 2 * tq), F32)],
        compiler_params=_params("arbitrary", "arbitrary"),
        name="causal_attention",
    )(*args, *extra)


def _merge_kernel(x_ref, mod_ref, gpre_ref, wg_ref, ya_ref, yb_ref, yc_ref, yd_ref,
                  woa_ref, wob_ref, woc_ref, wod_ref, wmix_ref, gpost_ref, o_ref):
    x = x_ref[0]
    shift = mod_ref[0, 0:1, :]
    scale = mod_ref[0, 1:2, :]
    gate = mod_ref[0, 2:3, :]
    h = (_rmsnorm(x, gpre_ref[...]) * (1.0 + scale) + shift).astype(BF16)
    merged = None
    branches = ((ya_ref, woa_ref), (yb_ref, wob_ref), (yc_ref, woc_ref), (yd_ref, wod_ref))
    for n, (pre_ref, wo_ref) in enumerate(branches):
        g = _sigmoid(_dot(h, wg_ref[:, n * D_MODEL:(n + 1) * D_MODEL]))
        term = g * _dot(pre_ref[0], wo_ref[...])
        merged = term if merged is None else merged + term
    z = _dot(merged.astype(BF16), wmix_ref[...])
    o_ref[0] = x + gate * _rmsnorm(z, gpost_ref[...])


def _merge(x, mod, ya, yb, yc, yd, w, l, *, tm):
    bsz, s, d = x.shape

    def tok(cols):
        return pl.BlockSpec((1, tm, cols), lambda b, t: (b, t, 0))

    consts_a = [w["g_pre_mix"], w["w_gates"]]
    consts_b = [w["da_w_o"], w["mla_w_o"], w["sc_w_o"], w["cf_w_o"], w["w_mix_out"],
                w["g_post_mix"]]
    in_specs = ([tok(d), _mod_spec(d, l)]
                + [_layer_spec(a, l) for a in consts_a]
                + [tok(a.shape[-1]) for a in (ya, yb, yc, yd)]
                + [_layer_spec(a, l) for a in consts_b])
    return pl.pallas_call(
        _merge_kernel,
        out_shape=jax.ShapeDtypeStruct(x.shape, F32),
        grid=(bsz, s // tm),
        in_specs=in_specs,
        out_specs=tok(d),
        compiler_params=_params("arbitrary", "arbitrary"),
        name="merge_out_proj",
    )(x, mod, *consts_a, ya, yb, yc, yd, *consts_b)


def _ffn_kernel(x_ref, mod_ref, gpre_ref, wup_ref, conv_ref, wdown_ref, gpost_ref, o_ref,
                abuf, carry, act_ref, *, tm):
    t = pl.program_id(1)

    @pl.when(t == 0)
    def _():
        carry[...] = jnp.zeros(carry.shape, F32)

    x = x_ref[0]
    shift = mod_ref[0, 3:4, :]
    scale = mod_ref[0, 4:5, :]
    gate = mod_ref[0, 5:6, :]
    h = (_rmsnorm(x, gpre_ref[...]) * (1.0 + scale) + shift).astype(BF16)
    fc = FFN_COL_CHUNK
    for c0 in range(0, FFN_DIM, fc):
        a = _dot(h, wup_ref[:, c0:c0 + fc])
        b = _dot(h, wup_ref[:, FFN_DIM + c0:FFN_DIM + c0 + fc])
        abuf[0:FFN_HALO, :] = carry[:, c0:c0 + fc]
        abuf[FFN_HALO:FFN_HALO + tm, :] = a
        carry[:, c0:c0 + fc] = a[tm - FFN_HALO:tm, :]
        cv = (conv_ref[2:3, c0:c0 + fc] * a
              + conv_ref[1:2, c0:c0 + fc] * abuf[FFN_HALO - 1:FFN_HALO - 1 + tm, :]
              + conv_ref[0:1, c0:c0 + fc] * abuf[FFN_HALO - 2:FFN_HALO - 2 + tm, :])
        act_ref[:, c0:c0 + fc] = (cv * _sigmoid(cv) * b).astype(BF16)
    y = _dot(act_ref[...], wdown_ref[...])
    o_ref[0] = x + gate * _rmsnorm(y, gpost_ref[...])


def _ffn(x, mod, w, l, *, tm):
    bsz, s, d = x.shape
    tok = pl.BlockSpec((1, tm, d), lambda b, t: (b, t, 0))
    consts = [w["g_pre_ffn"], w["w_up"], w["ffn_conv"], w["w_down"], w["g_post_ffn"]]
    return pl.pallas_call(
        functools.partial(_ffn_kernel, tm=tm),
        out_shape=jax.ShapeDtypeStruct(x.shape, F32),
        grid=(bsz, s // tm),
        in_specs=[tok, _mod_spec(d, l)] + [_layer_spec(a, l) for a in consts],
        out_specs=tok,
        scratch_shapes=[pltpu.VMEM((tm + FFN_HALO, FFN_COL_CHUNK), F32),
                        pltpu.VMEM((FFN_HALO, FFN_DIM), F32),
                        pltpu.VMEM((tm, FFN_DIM), BF16)],
        compiler_params=_params("arbitrary", "arbitrary"),
        name="gated_mlp",
    )(x, mod, *consts)


def _split_w_in(w_in):
    sizes = (DA_WIDTH, DA_WIDTH, DA_WIDTH, MLA_Q_LORA, MLA_KV_LORA, MLA_ROPE,
             SC_WIDTH, SC_WIDTH, SC_WIDTH, 2 * CF_WIDTH, N_BRANCHES * D_MODEL)
    out, acc = [], 0
    for sz in sizes:
        out.append(w_in[..., acc:acc + sz])
        acc += sz
    return out


def _prep_weights(p):
    depth = p["w_in"].shape[0]
    row = lambda a: a.reshape(depth, 1, -1)
    tr = lambda a: jnp.swapaxes(a, 1, 2)
    (w_q, w_k, w_v, w_cq, w_ckv, w_kr, w_u, w_b, w_c, w_cf, w_g) = _split_w_in(p["w_in"])
    d = w_q.shape[1]
    half = MLA_ROPE // 2
    pad_hi = HEAD_PAD - MLA_NOPE - MLA_ROPE
    z = lambda n: jnp.zeros((depth, d, n), F32)
    kr_lin = jnp.concatenate([z(MLA_NOPE), w_kr, z(pad_hi)], axis=2)
    kr_swp = jnp.concatenate([z(MLA_NOPE), w_kr[..., half:], w_kr[..., :half], z(pad_hi)], axis=2)
    w_nat = jnp.concatenate([w_u, w_b, w_c, w_cf, w_k, w_cq, w_ckv, kr_lin, kr_swp], axis=2)
    w_t = tr(jnp.concatenate([w_q, w_v], axis=2))

    w_uq = p["mla_w_uq"].reshape(depth, MLA_Q_LORA, MLA_HEADS, MLA_NOPE + MLA_ROPE)
    uq_pad = jnp.pad(w_uq, ((0, 0), (0, 0), (0, 0), (0, pad_hi)))
    rope = w_uq[..., MLA_NOPE:]
    uq_swp = jnp.concatenate([rope[..., half:], rope[..., :half]], axis=3)
    w_ukv = p["mla_w_ukv"].reshape(depth, MLA_KV_LORA, MLA_HEADS, MLA_NOPE + MLA_V)
    ukv_k = jnp.pad(w_ukv[..., :MLA_NOPE], ((0, 0), (0, 0), (0, 0), (0, HEAD_PAD - MLA_NOPE)))
    ukv_v = w_ukv[..., MLA_NOPE:]
    return {
        "g_pre_mix": row(p["g_pre_mix"]),
        "w_nat": w_nat.astype(BF16),
        "w_t": w_t.astype(BF16),
        "w_gates": w_g.astype(BF16),
        "g_cq": row(p["mla_g_cq"]),
        "wuq_t": tr(uq_pad.reshape(depth, MLA_Q_LORA, -1)).astype(BF16),
        "wuq_swp_t": tr(uq_swp.reshape(depth, MLA_Q_LORA, -1)).astype(BF16),
        "g_ckv": row(p["mla_g_ckv"]),
        "wukv_k": ukv_k.reshape(depth, MLA_KV_LORA, -1).astype(BF16),
        "wukv_v_t": tr(ukv_v.reshape(depth, MLA_KV_LORA, -1)).astype(BF16),
        "sc_conv": p["sc_conv"],
        "cf_conv": p["cf_conv"],
        "cf_conv_b": row(p["cf_conv_b"]),
        "cf_ln_g": row(p["cf_ln_g"]),
        "cf_ln_b": row(p["cf_ln_b"]),
        "lam": [row(p[n]) for n in ("da_lam_q1", "da_lam_k1", "da_lam_q2", "da_lam_k2")],
        "g_subln": p["da_g_subln"].reshape(depth, -1, 1),
        "da_w_o": p["da_w_o"].astype(BF16),
        "mla_w_o": p["mla_w_o"].astype(BF16),
        "sc_w_o": p["sc_w_o"].astype(BF16),
        "cf_w_o": p["cf_w_o"].astype(BF16),
        "w_mix_out": p["w_mix_out"].astype(BF16),
        "g_post_mix": row(p["g_post_mix"]),
        "g_pre_ffn": row(p["g_pre_ffn"]),
        "w_up": p["w_up"].astype(BF16),
        "ffn_conv": p["ffn_conv"],
        "w_down": p["w_down"].astype(BF16),
        "g_post_ffn": row(p["g_post_ffn"]),
    }


def kernel(x, c, positions, w_ada, b_ada, g_pre_mix, w_in, da_lam_q1, da_lam_k1, da_lam_q2, da_lam_k2, da_g_subln, da_w_o, mla_g_cq, mla_w_uq, mla_g_ckv, mla_w_ukv, mla_w_o, sc_conv, sc_w_o, cf_conv, cf_conv_b, cf_ln_g, cf_ln_b, cf_w_o, w_mix_out, g_post_mix, g_pre_ffn, w_up, ffn_conv, w_down, g_post_ffn):
    p = dict(g_pre_mix=g_pre_mix, w_in=w_in, da_lam_q1=da_lam_q1, da_lam_k1=da_lam_k1,
             da_lam_q2=da_lam_q2, da_lam_k2=da_lam_k2, da_g_subln=da_g_subln, da_w_o=da_w_o,
             mla_g_cq=mla_g_cq, mla_w_uq=mla_w_uq, mla_g_ckv=mla_g_ckv, mla_w_ukv=mla_w_ukv,
             mla_w_o=mla_w_o, sc_conv=sc_conv, sc_w_o=sc_w_o, cf_conv=cf_conv,
             cf_conv_b=cf_conv_b, cf_ln_g=cf_ln_g, cf_ln_b=cf_ln_b, cf_w_o=cf_w_o,
             w_mix_out=w_mix_out, g_post_mix=g_post_mix, g_pre_ffn=g_pre_ffn, w_up=w_up,
             ffn_conv=ffn_conv, w_down=w_down, g_post_ffn=g_post_ffn)
    bsz, s, _ = x.shape
    depth = w_ada.shape[0]
    tm = min(TOKEN_TILE, s)
    tm_mix = min(MIX_TOKEN_TILE, s)
    tq = min(ATTN_TILE, s)
    assert s % tm == 0 and tm % tq == 0 and s % tm_mix == 0

    half = MLA_ROPE // 2
    inv_freq = ROPE_THETA ** (-jnp.arange(half, dtype=F32) / half)
    invf = jnp.broadcast_to(inv_freq[:, None], (half, tm))
    positions3 = positions.reshape(bsz, 1, s)

    mod = _ada_modulation(c, w_ada, b_ada)
    w = _prep_weights(p)
    for l in range(depth):
        lam_init = 0.8 - 0.6 * math.exp(-0.3 * l)
        qt_da, k_da, vt_da, qt_m, k_m, vt_m, yc, yd = _in_proj(
            x, mod, positions3, invf, w, l, tm=tm, tq=tq)
        da_extra = w["lam"] + [w["g_subln"]]
        ya, yb = _attention(qt_da, k_da, vt_da, qt_m, k_m, vt_m, da_extra,
                            [_layer_spec(a, l) for a in da_extra], tq=tq, lam_init=lam_init)
        x = _merge(x, mod, ya, yb, yc, yd, w, l, tm=tm_mix)
        x = _ffn(x, mod, w, l, tm=tm_mix)
    return x
```

```python
import functools
import math

import jax
import jax.numpy as jnp
from jax import lax
from jax.experimental import pallas as pl
from jax.experimental.pallas import tpu as pltpu

D_MODEL = 1024
DA_HEADS = 4
DA_HEAD_DIM = 64
DA_WIDTH = DA_HEADS * 2 * DA_HEAD_DIM
MLA_HEADS = 8
MLA_Q_LORA = 384
MLA_KV_LORA = 256
MLA_NOPE = 64
MLA_ROPE = 32
MLA_V = 64
ROPE_THETA = 10000.0
SC_WIDTH = 512
SC_KERNEL = 3
CF_WIDTH = 512
CF_KERNEL = 31
N_BRANCHES = 4
FFN_DIM = 2816
FFN_KERNEL = 3
N_MOD = 6
NORM_EPS = 1e-6
LN_EPS = 1e-5

HEAD_PAD = 128
LOG2E = math.log2(math.e)
NEG_BIG = -1e30

VMEM_LIMIT_BYTES = 56 * 1024 * 1024
TOKEN_TILE = 512
MIX_TOKEN_TILE = 1024
ATTN_TILE = 256
SUBLANES = 8
CF_ROW_CHUNK = 32
FFN_COL_CHUNK = 256
SC_HALO = 8
CF_HALO = 32
FFN_HALO = 8
ONES_ROWS = 16
ATTN_GROUPS = 4

BF16 = jnp.bfloat16
F32 = jnp.float32


def _dot(a, b):
    return jnp.dot(a, b, preferred_element_type=F32)


def _dot_nt(a, b):
    return lax.dot_general(a, b, (((1,), (1,)), ((), ())), preferred_element_type=F32)


def _sigmoid(x):
    return 1.0 / (1.0 + jnp.exp(-x))


def _rmsnorm(x, g):
    return x * lax.rsqrt(jnp.mean(x * x, axis=-1, keepdims=True) + NORM_EPS) * g


def _const_spec(shape):
    nd = len(shape)
    return pl.BlockSpec(shape, lambda *_: (0,) * nd, pipeline_mode=pl.Buffered(1))


def _layer_spec(arr, l):
    nd = arr.ndim
    return pl.BlockSpec((None,) + arr.shape[1:], lambda *_: (l,) + (0,) * (nd - 1),
                        pipeline_mode=pl.Buffered(1))


def _mod_spec(d, l):
    return pl.BlockSpec((None, 1, N_MOD, d), lambda b, t: (l, b, 0, 0))


def _params(*sem):
    return pltpu.CompilerParams(dimension_semantics=sem, vmem_limit_bytes=VMEM_LIMIT_BYTES)


def _ada_kernel(c_ref, w_ref, b_ref, o_ref):
    c = c_ref[...]
    act = (c * _sigmoid(c)).astype(BF16)
    o_ref[0] = _dot(act, w_ref[0].astype(BF16)) + b_ref[0]


def _ada_modulation(c, w_ada, b_ada):
    depth, d, n = w_ada.shape
    bsz = c.shape[0]
    out = pl.pallas_call(
        _ada_kernel,
        out_shape=jax.ShapeDtypeStruct((depth, bsz, n), F32),
        grid=(depth, n // d),
        in_specs=[
            pl.BlockSpec((bsz, d), lambda l, j: (0, 0)),
            pl.BlockSpec((1, d, d), lambda l, j: (l, 0, j)),
            pl.BlockSpec((1, 1, d), lambda l, j: (l, 0, j)),
        ],
        out_specs=pl.BlockSpec((1, bsz, d), lambda l, j: (l, 0, j)),
        compiler_params=_params("arbitrary", "arbitrary"),
        name="ada_modulation",
    )(c, w_ada, b_ada.reshape(depth, 1, n))
    return out.reshape(depth, bsz, N_MOD, d)


_C_SC = 0
_C_CF = _C_SC + 3 * SC_WIDTH
_C_DAK = _C_CF + 2 * CF_WIDTH
_C_CQ = _C_DAK + DA_WIDTH
_C_END = _C_CQ + MLA_Q_LORA + MLA_KV_LORA + 2 * HEAD_PAD


def _in_proj_kernel(x_ref, mod_ref, gpre_ref, pos_ref, invf_ref, wnat_ref, wt_ref,
                    gcq_ref, wuqt_ref, wuqswt_ref, gckv_ref, wukvk_ref, wukvvt_ref,
                    scconv_ref, cfconv_ref, cfb_ref, lng_ref, lnb_ref,
                    qtda_ref, kda_ref, vtda_ref, qtm_ref, km_ref, vtm_ref, yc_ref, yd_ref,
                    wbuf, ubuf, ushift, *, tm, tq):
    t = pl.program_id(1)

    @pl.when(t == 0)
    def _():
        wbuf[0:SC_HALO, :] = jnp.zeros((SC_HALO, SC_WIDTH), F32)
        ubuf[0:CF_HALO, :] = jnp.zeros((CF_HALO, CF_WIDTH), F32)

    x = x_ref[0]
    shift = mod_ref[0, 0:1, :]
    scale = mod_ref[0, 1:2, :]
    h = (_rmsnorm(x, gpre_ref[...]) * (1.0 + scale) + shift).astype(BF16)

    cf = _dot(h, wnat_ref[:, _C_CF:_C_DAK])
    ubuf[CF_HALO:CF_HALO + tm, :] = cf[:, 0:CF_WIDTH] * _sigmoid(cf[:, CF_WIDTH:2 * CF_WIDTH])

    def conv_work():
        for b in range(1, SUBLANES):
            ushift[b - 1] = pltpu.roll(ubuf[...], b, axis=0)
            yield
        for r0 in range(0, tm, CF_ROW_CHUNK):
            acc = jnp.broadcast_to(cfb_ref[...], (CF_ROW_CHUNK, CF_WIDTH))
            for s in range(CF_KERNEL):
                a, b = divmod(s, SUBLANES)
                lo = CF_HALO + r0 - SUBLANES * a
                src = (ubuf[lo:lo + CF_ROW_CHUNK, :] if b == 0
                       else ushift[b - 1, lo:lo + CF_ROW_CHUNK, :])
                acc = acc + cfconv_ref[CF_KERNEL - 1 - s:CF_KERNEL - s, :] * src
            mu = jnp.mean(acc, axis=-1, keepdims=True)
            cen = acc - mu
            var = jnp.mean(cen * cen, axis=-1, keepdims=True)
            y = cen * lax.rsqrt(var + LN_EPS) * lng_ref[...] + lnb_ref[...]
            yd_ref[0, r0:r0 + CF_ROW_CHUNK, :] = (y * _sigmoid(y)).astype(BF16)
            yield
        ubuf[0:CF_HALO, :] = ubuf[tm:tm + CF_HALO, :]

    conv_units = conv_work()

    def advance(n):
        for _ in range(n):
            next(conv_units, None)

    sc_u = _dot(h, wnat_ref[:, _C_SC:_C_SC + SC_WIDTH])
    advance(2)
    sc_c = _dot(h, wnat_ref[:, _C_SC + 2 * SC_WIDTH:_C_CF])
    w = sc_c * sc_u
    wbuf[SC_HALO:SC_HALO + tm, :] = w
    advance(2)
    sc_b = _dot(h, wnat_ref[:, _C_SC + SC_WIDTH:_C_SC + 2 * SC_WIDTH])
    conv = (scconv_ref[2:3, :] * w
            + scconv_ref[1:2, :] * wbuf[SC_HALO - 1:SC_HALO - 1 + tm, :]
            + scconv_ref[0:1, :] * wbuf[SC_HALO - 2:SC_HALO - 2 + tm, :])
    yc_ref[0] = (sc_b * conv).astype(BF16)
    wbuf[0:SC_HALO, :] = wbuf[tm:tm + SC_HALO, :]
    advance(2)

    kda_ref[0] = _dot(h, wnat_ref[:, _C_DAK:_C_CQ]).astype(BF16)
    advance(3)
    qs_da = DA_HEAD_DIM ** -0.5 * LOG2E
    half_w = DA_WIDTH // 2
    for r0 in (0, half_w):
        tr_q = _dot_nt(wt_ref[r0:r0 + half_w, :], h)
        for c in range(tm // tq):
            qtda_ref[0, c, r0:r0 + half_w, :] = (tr_q[:, c * tq:(c + 1) * tq] * qs_da).astype(BF16)
        advance(1)
    for r0 in (0, half_w):
        tr_v = _dot_nt(wt_ref[DA_WIDTH + r0:DA_WIDTH + r0 + half_w, :], h)
        for c in range(tm // tq):
            vtda_ref[0, c, r0:r0 + half_w, :] = tr_v[:, c * tq:(c + 1) * tq].astype(BF16)
        advance(1)

    lat = _dot(h, wnat_ref[:, _C_CQ:_C_END])
    o_kv = MLA_Q_LORA
    o_kr = o_kv + MLA_KV_LORA
    cqn = _rmsnorm(lat[:, 0:o_kv], gcq_ref[...]).astype(BF16)
    ckvn = _rmsnorm(lat[:, o_kv:o_kr], gckv_ref[...]).astype(BF16)
    kr_lin = lat[:, o_kr:o_kr + HEAD_PAD]
    kr_swp = lat[:, o_kr + HEAD_PAD:o_kr + 2 * HEAD_PAD]
    advance(2)

    half = MLA_ROPE // 2
    ang = invf_ref[...] * pos_ref[0].astype(F32)
    cos_h = jnp.cos(ang)
    sin_h = jnp.sin(ang)
    cos_r = jnp.concatenate([cos_h, cos_h], axis=0)
    sin_r = jnp.concatenate([-sin_h, sin_h], axis=0)

    zeros_lo = jnp.zeros((MLA_NOPE, tm), F32)
    zeros_hi = jnp.zeros((HEAD_PAD - MLA_NOPE - MLA_ROPE, tm), F32)
    cos_n = jnp.concatenate([zeros_lo, cos_r, zeros_hi], axis=0).T
    sin_n = jnp.concatenate([zeros_lo, sin_r, zeros_hi], axis=0).T
    k_rope = kr_lin * cos_n + kr_swp * sin_n
    advance(1)
    k_nope = _dot(ckvn, wukvk_ref[...])
    for hd in range(MLA_HEADS):
        km_ref[0, :, hd * HEAD_PAD:(hd + 1) * HEAD_PAD] = (
            k_nope[:, hd * HEAD_PAD:(hd + 1) * HEAD_PAD] + k_rope).astype(BF16)
    advance(3)

    vt = _dot_nt(wukvvt_ref[...], ckvn)
    for c in range(tm // tq):
        vtm_ref[0, c] = vt[:, c * tq:(c + 1) * tq].astype(BF16)
    advance(2)
    half_q = MLA_HEADS * HEAD_PAD // 2
    qt_lin_lo = _dot_nt(wuqt_ref[0:half_q, :], cqn)
    advance(2)
    qt_lin_hi = _dot_nt(wuqt_ref[half_q:2 * half_q, :], cqn)
    advance(1)
    qt_lin = jnp.concatenate([qt_lin_lo, qt_lin_hi], axis=0)
    qt_swp = _dot_nt(wuqswt_ref[...], cqn)
    qs_mla = (MLA_NOPE + MLA_ROPE) ** -0.5 * LOG2E
    pad_rows = jnp.zeros((HEAD_PAD - MLA_NOPE - MLA_ROPE, tm), F32)
    qt_heads = []
    for hd in range(MLA_HEADS):
        lin = qt_lin[hd * HEAD_PAD:(hd + 1) * HEAD_PAD]
        rot = (lin[MLA_NOPE:MLA_NOPE + MLA_ROPE] * cos_r
               + qt_swp[hd * MLA_ROPE:(hd + 1) * MLA_ROPE] * sin_r)
        qt_heads.append(jnp.concatenate([lin[0:MLA_NOPE], rot, pad_rows], axis=0))
    qt = jnp.concatenate(qt_heads, axis=0) * qs_mla
    for c in range(tm // tq):
        qtm_ref[0, c] = qt[:, c * tq:(c + 1) * tq].astype(BF16)
    advance(CF_KERNEL)


def _in_proj(x, mod, positions3, invf, w, l, *, tm, tq):
    bsz, s, d = x.shape
    nt = s // tm
    nq = s // tq
    kern = functools.partial(_in_proj_kernel, tm=tm, tq=tq)
    names = ["g_pre_mix", None, None, "w_nat", "w_t", "g_cq", "wuq_t", "wuq_swp_t", "g_ckv",
             "wukv_k", "wukv_v_t", "sc_conv", "cf_conv", "cf_conv_b", "cf_ln_g", "cf_ln_b"]
    in_specs = [pl.BlockSpec((1, tm, d), lambda b, t: (b, t, 0)), _mod_spec(d, l)]
    args = [x, mod]
    fixed = iter([(positions3, pl.BlockSpec((1, 1, tm), lambda b, t: (b, 0, t))),
                  (invf, _const_spec(invf.shape))])
    for n in names:
        arr, spec = next(fixed) if n is None else (w[n], _layer_spec(w[n], l))
        in_specs.append(spec)
        args.append(arr)
    slab = tm // tq
    out_shape = (
        jax.ShapeDtypeStruct((bsz, nq, DA_WIDTH, tq), BF16),
        jax.ShapeDtypeStruct((bsz, s, DA_WIDTH), BF16),
        jax.ShapeDtypeStruct((bsz, nq, DA_WIDTH, tq), BF16),
        jax.ShapeDtypeStruct((bsz, nq, MLA_HEADS * HEAD_PAD, tq), BF16),
        jax.ShapeDtypeStruct((bsz, s, MLA_HEADS * HEAD_PAD), BF16),
        jax.ShapeDtypeStruct((bsz, nq, MLA_HEADS * MLA_V, tq), BF16),
        jax.ShapeDtypeStruct((bsz, s, SC_WIDTH), BF16),
        jax.ShapeDtypeStruct((bsz, s, CF_WIDTH), BF16),
    )

    def tspec(rows):
        return pl.BlockSpec((1, slab, rows, tq), lambda b, t: (b, t, 0, 0))

    def nspec(cols):
        return pl.BlockSpec((1, tm, cols), lambda b, t: (b, t, 0))

    out_specs = (tspec(DA_WIDTH), nspec(DA_WIDTH), tspec(DA_WIDTH),
                 tspec(MLA_HEADS * HEAD_PAD), nspec(MLA_HEADS * HEAD_PAD),
                 tspec(MLA_HEADS * MLA_V), nspec(SC_WIDTH), nspec(CF_WIDTH))
    return pl.pallas_call(
        kern,
        out_shape=out_shape,
        grid=(bsz, nt),
        in_specs=in_specs,
        out_specs=out_specs,
        scratch_shapes=[pltpu.VMEM((tm + SC_HALO, SC_WIDTH), F32),
                        pltpu.VMEM((tm + CF_HALO, CF_WIDTH), F32),
                        pltpu.VMEM((SUBLANES - 1, tm + CF_HALO, CF_WIDTH), F32)],
        compiler_params=_params("arbitrary", "arbitrary"),
        name="in_proj",
    )(*args)


def _attn_kernel(qd_ref, qdn_ref, kd_ref, vd_ref, qm_ref, qmn_ref, km_ref, vm_ref,
                 lq1_ref, lk1_ref, lq2_ref, lk2_ref, gsub_ref, od_ref, om_ref,
                 qcat_ref, m_ref, l_ref, acc_ref, s_ref, smax_ref, *, tq, lam_init):
    i = pl.program_id(1)
    tk = tq
    n_groups = 2 * ATTN_GROUPS

    for g in range(n_groups):
        m_ref[g] = jnp.full(m_ref.shape[1:], NEG_BIG, F32)
        l_ref[g] = jnp.zeros(l_ref.shape[1:], F32)
        acc_ref[g] = jnp.zeros(acc_ref.shape[1:], F32)

    def load_queries(src_ref, buf):
        for g in range(ATTN_GROUPS):
            qt = src_ref[0, 0, g * HEAD_PAD:(g + 1) * HEAD_PAD, :]
            row = lax.broadcasted_iota(jnp.int32, qt.shape, 0)
            zero = jnp.zeros_like(qt)
            qcat_ref[buf, g, :, 0:tq] = jnp.where(row < DA_HEAD_DIM, qt, zero)
            qcat_ref[buf, g, :, tq:2 * tq] = jnp.where(row >= DA_HEAD_DIM, qt, zero)

    all_groups = tuple(range(n_groups))

    def scores(j, slot, qbuf, src_ref=qm_ref, groups=all_groups):
        k0 = pl.multiple_of(j * tk, tk)
        for g in groups:
            if g < ATTN_GROUPS:
                c0 = g * HEAD_PAD
                s = _dot(kd_ref[0, pl.ds(k0, tk), c0:c0 + HEAD_PAD], qcat_ref[qbuf, g])
                s_ref[slot, g] = s
                smax_ref[slot, g] = jnp.max(s, axis=0, keepdims=True)
                continue
            for n in range(2):
                r0 = (2 * (g - ATTN_GROUPS) + n) * HEAD_PAD
                s = _dot(km_ref[0, pl.ds(k0, tk), r0:r0 + HEAD_PAD],
                         src_ref[0, 0, r0:r0 + HEAD_PAD, :])
                s_ref[slot, g, :, n * tq:(n + 1) * tq] = s
                smax_ref[slot, g, :, n * tq:(n + 1) * tq] = jnp.max(s, axis=0, keepdims=True)

    def softmax_pv(j, slot, masked, groups=all_groups):
        for g in groups:
            v_ref, v0 = (vd_ref, g * HEAD_PAD) if g < ATTN_GROUPS else (
                vm_ref, (g - ATTN_GROUPS) * HEAD_PAD)
            s = s_ref[slot, g]
            if masked:
                kpos = lax.broadcasted_iota(jnp.int32, (tk, tq), 0)
                qpos = lax.broadcasted_iota(jnp.int32, (tk, tq), 1)
                keep = kpos <= qpos
                keep = jnp.concatenate([keep, keep], axis=1)
                s = jnp.where(keep, s, NEG_BIG)
                smax = jnp.max(s, axis=0, keepdims=True)
            else:
                smax = smax_ref[slot, g]
            m_old = m_ref[g]
            m_new = jnp.maximum(m_old, smax)
            alpha = jnp.exp2(m_old - m_new)
            p = jnp.exp2(s - m_new).astype(BF16)
            m_ref[g] = m_new
            vt = jnp.concatenate([v_ref[0, j, v0:v0 + HEAD_PAD, :],
                                  jnp.ones((ONES_ROWS, tk), BF16)], axis=0)
            pv = _dot(vt, p)
            acc_ref[g] = alpha * acc_ref[g] + pv[0:HEAD_PAD]
            l_ref[g] = alpha * l_ref[g] + pv[HEAD_PAD:HEAD_PAD + 1]

    @pl.when(i == 0)
    def _():
        load_queries(qd_ref, 0)
        scores(0, 0, 0)

    def body(jj, carry):
        j = 2 * jj
        for g in all_groups:
            scores(j + 1, 1, i % 2, groups=(g,))
            softmax_pv(j, 0, False, groups=(g,))
        for g in all_groups:
            scores(j + 2, 0, i % 2, groups=(g,))
            softmax_pv(j + 1, 1, False, groups=(g,))
        return carry

    lax.fori_loop(0, i // 2, body, 0)

    def finalize(g):
        acc = acc_ref[g]
        l = l_ref[g]
        if g < ATTN_GROUPS:
            lam = (jnp.exp(jnp.sum(lq1_ref[...] * lk1_ref[...], axis=-1, keepdims=True))
                   - jnp.exp(jnp.sum(lq2_ref[...] * lk2_ref[...], axis=-1, keepdims=True))
                   + lam_init)
            o = acc[:, 0:tq] / l[:, 0:tq] - lam * (acc[:, tq:2 * tq] / l[:, tq:2 * tq])
            o = o * lax.rsqrt(jnp.mean(o * o, axis=0, keepdims=True) + NORM_EPS)
            o = o * gsub_ref[...] * (1.0 - lam_init)
            od_ref[0, :, g * HEAD_PAD:(g + 1) * HEAD_PAD] = o.T.astype(BF16)
        else:
            o = jnp.concatenate([acc[0:MLA_V, 0:tq] / l[:, 0:tq],
                                 acc[MLA_V:2 * MLA_V, tq:2 * tq] / l[:, tq:2 * tq]], axis=0)
            c0 = (g - ATTN_GROUPS) * HEAD_PAD
            om_ref[0, :, c0:c0 + HEAD_PAD] = o.T.astype(BF16)

    def epilogue(slot):
        load_queries(qdn_ref, 1 - slot)
        for g in all_groups:
            softmax_pv(i, slot, True, groups=(g,))
            scores(0, 0, 1 - slot, qmn_ref, groups=(g,))
            finalize(g)

    @pl.when(i % 2 == 0)
    def _():
        epilogue(0)

    @pl.when(i % 2 == 1)
    def _():
        for g in all_groups:
            scores(i, 1, 1, groups=(g,))
            softmax_pv(i - 1, 0, False, groups=(g,))
        epilogue(1)


def _attention(qt_da, k_da, vt_da, qt_m, k_m, vt_m, extra, extra_specs, *, tq, lam_init):
    bsz, nq, _, _ = qt_da.shape
    s = k_da.shape[1]
    width = ATTN_GROUPS * HEAD_PAD
    n_groups = 2 * ATTN_GROUPS
    kern = functools.partial(_attn_kernel, tq=tq, lam_init=lam_init)
    in_specs, args = [], []
    for qt, k, vt in ((qt_da, k_da, vt_da), (qt_m, k_m, vt_m)):
        rows = qt.shape[2]
        in_specs += [
            pl.BlockSpec((1, 1, rows, tq), lambda b, i: (b, i, 0, 0)),
            pl.BlockSpec((1, 1, rows, tq), lambda b, i: (b, jnp.minimum(i + 1, nq - 1), 0, 0)),
            pl.BlockSpec((1, s, rows), lambda b, i: (b, 0, 0)),
            pl.BlockSpec((1, nq, width, tq), lambda b, i: (b, 0, 0, 0)),
        ]
        args += [qt, qt, k, vt]
    in_specs += extra_specs
    out = jax.ShapeDtypeStruct((bsz, s, width), BF16)
    out_spec = pl.BlockSpec((1, tq, width), lambda b, i: (b, i, 0))
    return pl.pallas_call(
        kern,
        out_shape=(out, out),
        grid=(bsz, nq),
        in_specs=in_specs,
        out_specs=(out_spec, out_spec),
        scratch_shapes=[pltpu.VMEM((2, ATTN_GROUPS, HEAD_PAD, 2 * tq), BF16),
                        pltpu.VMEM((n_groups, 1, 2 * tq), F32),
                        pltpu.VMEM((n_groups, 1, 2 * tq), F32),
                        pltpu.VMEM((n_groups, HEAD_PAD, 2 * tq), F32),
                        pltpu.VMEM((2, n_groups, tq, 2 * tq), F32),
                        pltpu.VMEM((2, n_groups, 1, 2 * tq), F32)],
        compiler_params=_params("arbitrary", "arbitrary"),
        name="causal_attention",
    )(*args, *extra)


def _merge_kernel(x_ref, mod_ref, gpre_ref, wg_ref, ya_ref, yb_ref, yc_ref, yd_ref,
                  woa_ref, wob_ref, woc_ref, wod_ref, wmix_ref, gpost_ref, o_ref):
    x = x_ref[0]
    shift = mod_ref[0, 0:1, :]
    scale = mod_ref[0, 1:2, :]
    gate = mod_ref[0, 2:3, :]
    h = (_rmsnorm(x, gpre_ref[...]) * (1.0 + scale) + shift).astype(BF16)
    merged = None
    branches = ((ya_ref, woa_ref), (yb_ref, wob_ref), (yc_ref, woc_ref), (yd_ref, wod_ref))
    for n, (pre_ref, wo_ref) in enumerate(branches):
        g = _sigmoid(_dot(h, wg_ref[:, n * D_MODEL:(n + 1) * D_MODEL]))
        term = g * _dot(pre_ref[0], wo_ref[...])
        merged = term if merged is None else merged + term
    z = _dot(merged.astype(BF16), wmix_ref[...])
    o_ref[0] = x + gate * _rmsnorm(z, gpost_ref[...])


def _merge(x, mod, ya, yb, yc, yd, w, l, *, tm):
    bsz, s, d = x.shape

    def tok(cols):
        return pl.BlockSpec((1, tm, cols), lambda b, t: (b, t, 0))

    consts_a = [w["g_pre_mix"], w["w_gates"]]
    consts_b = [w["da_w_o"], w["mla_w_o"], w["sc_w_o"], w["cf_w_o"], w["w_mix_out"],
                w["g_post_mix"]]
    in_specs = ([tok(d), _mod_spec(d, l)]
                + [_layer_spec(a, l) for a in consts_a]
                + [tok(a.shape[-1]) for a in (ya, yb, yc, yd)]
                + [_layer_spec(a, l) for a in consts_b])
    return pl.pallas_call(
        _merge_kernel,
        out_shape=jax.ShapeDtypeStruct(x.shape, F32),
        grid=(bsz, s // tm),
        in_specs=in_specs,
        out_specs=tok(d),
        compiler_params=_params("arbitrary", "arbitrary"),
        name="merge_out_proj",
    )(x, mod, *consts_a, ya, yb, yc, yd, *consts_b)


def _ffn_kernel(x_ref, mod_ref, gpre_ref, wup_ref, conv_ref, wdown_ref, gpost_ref, o_ref,
                abuf, carry, act_ref, *, tm):
    t = pl.program_id(1)

    @pl.when(t == 0)
    def _():
        carry[...] = jnp.zeros(carry.shape, F32)

    x = x_ref[0]
    shift = mod_ref[0, 3:4, :]
    scale = mod_ref[0, 4:5, :]
    gate = mod_ref[0, 5:6, :]
    h = (_rmsnorm(x, gpre_ref[...]) * (1.0 + scale) + shift).astype(BF16)
    fc = FFN_COL_CHUNK
    for c0 in range(0, FFN_DIM, fc):
        a = _dot(h, wup_ref[:, c0:c0 + fc])
        b = _dot(h, wup_ref[:, FFN_DIM + c0:FFN_DIM + c0 + fc])
        abuf[0:FFN_HALO, :] = carry[:, c0:c0 + fc]
        abuf[FFN_HALO:FFN_HALO + tm, :] = a
        carry[:, c0:c0 + fc] = a[tm - FFN_HALO:tm, :]
        cv = (conv_ref[2:3, c0:c0 + fc] * a
              + conv_ref[1:2, c0:c0 + fc] * abuf[FFN_HALO - 1:FFN_HALO - 1 + tm, :]
              + conv_ref[0:1, c0:c0 + fc] * abuf[FFN_HALO - 2:FFN_HALO - 2 + tm, :])
        act_ref[:, c0:c0 + fc] = (cv * _sigmoid(cv) * b).astype(BF16)
    y = _dot(act_ref[...], wdown_ref[...])
    o_ref[0] = x + gate * _rmsnorm(y, gpost_ref[...])


def _ffn(x, mod, w, l, *, tm):
    bsz, s, d = x.shape
    tok = pl.BlockSpec((1, tm, d), lambda b, t: (b, t, 0))
    consts = [w["g_pre_ffn"], w["w_up"], w["ffn_conv"], w["w_down"], w["g_post_ffn"]]
    return pl.pallas_call(
        functools.partial(_ffn_kernel, tm=tm),
        out_shape=jax.ShapeDtypeStruct(x.shape, F32),
        grid=(bsz, s // tm),
        in_specs=[tok, _mod_spec(d, l)] + [_layer_spec(a, l) for a in consts],
        out_specs=tok,
        scratch_shapes=[pltpu.VMEM((tm + FFN_HALO, FFN_COL_CHUNK), F32),
                        pltpu.VMEM((FFN_HALO, FFN_DIM), F32),
                        pltpu.VMEM((tm, FFN_DIM), BF16)],
        compiler_params=_params("arbitrary", "arbitrary"),
        name="gated_mlp",
    )(x, mod, *consts)


def _split_w_in(w_in):
    sizes = (DA_WIDTH, DA_WIDTH, DA_WIDTH, MLA_Q_LORA, MLA_KV_LORA, MLA_ROPE,
             SC_WIDTH, SC_WIDTH, SC_WIDTH, 2 * CF_WIDTH, N_BRANCHES * D_MODEL)
    out, acc = [], 0
    for sz in sizes:
        out.append(w_in[..., acc:acc + sz])
        acc += sz
    return out


def _prep_weights(p):
    depth = p["w_in"].shape[0]
    row = lambda a: a.reshape(depth, 1, -1)
    tr = lambda a: jnp.swapaxes(a, 1, 2)
    (w_q, w_k, w_v, w_cq, w_ckv, w_kr, w_u, w_b, w_c, w_cf, w_g) = _split_w_in(p["w_in"])
    d = w_q.shape[1]
    half = MLA_ROPE // 2
    pad_hi = HEAD_PAD - MLA_NOPE - MLA_ROPE
    z = lambda n: jnp.zeros((depth, d, n), F32)
    kr_lin = jnp.concatenate([z(MLA_NOPE), w_kr, z(pad_hi)], axis=2)
    kr_swp = jnp.concatenate([z(MLA_NOPE), w_kr[..., half:], w_kr[..., :half], z(pad_hi)], axis=2)
    w_nat = jnp.concatenate([w_u, w_b, w_c, w_cf, w_k, w_cq, w_ckv, kr_lin, kr_swp], axis=2)
    w_t = tr(jnp.concatenate([w_q, w_v], axis=2))

    w_uq = p["mla_w_uq"].reshape(depth, MLA_Q_LORA, MLA_HEADS, MLA_NOPE + MLA_ROPE)
    uq_pad = jnp.pad(w_uq, ((0, 0), (0, 0), (0, 0), (0, pad_hi)))
    rope = w_uq[..., MLA_NOPE:]
    uq_swp = jnp.concatenate([rope[..., half:], rope[..., :half]], axis=3)
    w_ukv = p["mla_w_ukv"].reshape(depth, MLA_KV_LORA, MLA_HEADS, MLA_NOPE + MLA_V)
    ukv_k = jnp.pad(w_ukv[..., :MLA_NOPE], ((0, 0), (0, 0), (0, 0), (0, HEAD_PAD - MLA_NOPE)))
    ukv_v = w_ukv[..., MLA_NOPE:]
    return {
        "g_pre_mix": row(p["g_pre_mix"]),
        "w_nat": w_nat.astype(BF16),
        "w_t": w_t.astype(BF16),
        "w_gates": w_g.astype(BF16),
        "g_cq": row(p["mla_g_cq"]),
        "wuq_t": tr(uq_pad.reshape(depth, MLA_Q_LORA, -1)).astype(BF16),
        "wuq_swp_t": tr(uq_swp.reshape(depth, MLA_Q_LORA, -1)).astype(BF16),
        "g_ckv": row(p["mla_g_ckv"]),
        "wukv_k": ukv_k.reshape(depth, MLA_KV_LORA, -1).astype(BF16),
        "wukv_v_t": tr(ukv_v.reshape(depth, MLA_KV_LORA, -1)).astype(BF16),
        "sc_conv": p["sc_conv"],
        "cf_conv": p["cf_conv"],
        "cf_conv_b": row(p["cf_conv_b"]),
        "cf_ln_g": row(p["cf_ln_g"]),
        "cf_ln_b": row(p["cf_ln_b"]),
        "lam": [row(p[n]) for n in ("da_lam_q1", "da_lam_k1", "da_lam_q2", "da_lam_k2")],
        "g_subln": p["da_g_subln"].reshape(depth, -1, 1),
        "da_w_o": p["da_w_o"].astype(BF16),
        "mla_w_o": p["mla_w_o"].astype(BF16),
        "sc_w_o": p["sc_w_o"].astype(BF16),
        "cf_w_o": p["cf_w_o"].astype(BF16),
        "w_mix_out": p["w_mix_out"].astype(BF16),
        "g_post_mix": row(p["g_post_mix"]),
        "g_pre_ffn": row(p["g_pre_ffn"]),
        "w_up": p["w_up"].astype(BF16),
        "ffn_conv": p["ffn_conv"],
        "w_down": p["w_down"].astype(BF16),
        "g_post_ffn": row(p["g_post_ffn"]),
    }


def kernel(x, c, positions, w_ada, b_ada, g_pre_mix, w_in, da_lam_q1, da_lam_k1, da_lam_q2, da_lam_k2, da_g_subln, da_w_o, mla_g_cq, mla_w_uq, mla_g_ckv, mla_w_ukv, mla_w_o, sc_conv, sc_w_o, cf_conv, cf_conv_b, cf_ln_g, cf_ln_b, cf_w_o, w_mix_out, g_post_mix, g_pre_ffn, w_up, ffn_conv, w_down, g_post_ffn):
    p = dict(g_pre_mix=g_pre_mix, w_in=w_in, da_lam_q1=da_lam_q1, da_lam_k1=da_lam_k1,
             da_lam_q2=da_lam_q2, da_lam_k2=da_lam_k2, da_g_subln=da_g_subln, da_w_o=da_w_o,
             mla_g_cq=mla_g_cq, mla_w_uq=mla_w_uq, mla_g_ckv=mla_g_ckv, mla_w_ukv=mla_w_ukv,
             mla_w_o=mla_w_o, sc_conv=sc_conv, sc_w_o=sc_w_o, cf_conv=cf_conv,
             cf_conv_b=cf_conv_b, cf_ln_g=cf_ln_g, cf_ln_b=cf_ln_b, cf_w_o=cf_w_o,
             w_mix_out=w_mix_out, g_post_mix=g_post_mix, g_pre_ffn=g_pre_ffn, w_up=w_up,
             ffn_conv=ffn_conv, w_down=w_down, g_post_ffn=g_post_ffn)
    bsz, s, _ = x.shape
    depth = w_ada.shape[0]
    tm = min(TOKEN_TILE, s)
    tm_mix = min(MIX_TOKEN_TILE, s)
    tq = min(ATTN_TILE, s)
    assert s % tm == 0 and tm % tq == 0 and s % tm_mix == 0

    half = MLA_ROPE // 2
    inv_freq = ROPE_THETA ** (-jnp.arange(half, dtype=F32) / half)
    invf = jnp.broadcast_to(inv_freq[:, None], (half, tm))
    positions3 = positions.reshape(bsz, 1, s)

    mod = _ada_modulation(c, w_ada, b_ada)
    w = _prep_weights(p)
    for l in range(depth):
        lam_init = 0.8 - 0.6 * math.exp(-0.3 * l)
        qt_da, k_da, vt_da, qt_m, k_m, vt_m, yc, yd = _in_proj(
            x, mod, positions3, invf, w, l, tm=tm, tq=tq)
        da_extra = w["lam"] + [w["g_subln"]]
        ya, yb = _attention(qt_da, k_da, vt_da, qt_m, k_m, vt_m, da_extra,
                            [_layer_spec(a, l) for a in da_extra], tq=tq, lam_init=lam_init)
        x = _merge(x, mod, ya, yb, yc, yd, w, l, tm=tm_mix)
        x = _ffn(x, mod, w, l, tm=tm_mix)
    return x
```
